```python
import jax, jax.numpy as jnp
from jax import lax
import numpy as np

D_MODEL = 2048
BATCH = 4
SEQ = 2048
DEPTH = 1

PLE_DIM = 256
CONV_DIM = 1024
CONV_K = 3
MLSTM_HEADS = 4
MLSTM_HEAD_DIM = 256
MLSTM_DIM = MLSTM_HEADS * MLSTM_HEAD_DIM
MLSTM_CHUNK = 128
MIX_DIM = CONV_DIM + MLSTM_DIM
IN_COLS = 3 * CONV_DIM + 4 * MLSTM_DIM + 2 * MLSTM_HEADS
PEER_HEADS = 8
PEER_N_KEYS = 128
PEER_N_EXPERTS = PEER_N_KEYS * PEER_N_KEYS
PEER_KEY_DIM = 256
PEER_HALF = PEER_KEY_DIM // 2
PEER_TOPK = 16
PEER_TOKEN_BLOCK = 128
EPS = 1e-6

kernel_name = "hybrid_conv_mlstm_peer_block"


def rmsnorm(x, g):
    xf = x.astype(jnp.float32)
    r = lax.rsqrt(jnp.mean(xf * xf, axis=-1, keepdims=True) + EPS)
    return (xf * r).astype(x.dtype) * g


def short_conv(b_gate, c_gate, u, w):
    z = c_gate * u
    s = z.shape[1]
    zp = jnp.pad(z, ((0, 0), (CONV_K - 1, 0), (0, 0)))
    y = w[0] * zp[:, 0:s]
    for j in range(1, CONV_K):
        y = y + w[j] * zp[:, j:j + s]
    return b_gate * y


def mlstm_chunkwise(q, k, v, i_pre, f_pre):
    bsz, s, nh, d = q.shape
    L = MLSTM_CHUNK
    nc = s // L

    def to_chunks(t):
        t = t.reshape((bsz, nc, L, nh) + t.shape[3:])
        return jnp.moveaxis(t, (1, 3), (0, 2))

    f32 = jnp.float32
    qc = to_chunks(q.astype(f32))
    kc = to_chunks(k.astype(f32)) * (d ** -0.5)
    vc = to_chunks(v.astype(f32))
    ic = to_chunks(i_pre.astype(f32))
    lfc = to_chunks(jax.nn.log_sigmoid(f_pre.astype(f32)))
    causal = jnp.tril(jnp.ones((L, L), dtype=bool))

    def step(carry, inp):
        C, n, m = carry
        qt, kt, vt, it, lft = inp
        b = jnp.cumsum(lft, axis=-1)
        dmat = jnp.where(causal, b[..., :, None] - b[..., None, :] + it[..., None, :], -jnp.inf)
        inter = b + m[..., None]
        m_t = jnp.maximum(inter, jnp.max(dmat, axis=-1))
        pw = jnp.exp(dmat - m_t[..., None])
        sc = jnp.einsum('bhtd,bhsd->bhts', qt, kt) * pw
        a = jnp.exp(inter - m_t)
        num = a[..., None] * jnp.einsum('bhtd,bhde->bhte', qt, C) + jnp.einsum('bhts,bhse->bhte', sc, vt)
        den = a * jnp.einsum('bhtd,bhd->bht', qt, n) + jnp.sum(sc, axis=-1)
        h = num / jnp.maximum(jnp.abs(den), jnp.exp(-m_t))[..., None]
        m_new = m_t[..., -1]
        w_s = jnp.exp(b[..., -1:] - b + it - m_new[..., None])
        a_l = jnp.exp(b[..., -1] + m - m_new)
        C_new = a_l[..., None, None] * C + jnp.einsum('bhs,bhsd,bhse->bhde', w_s, kt, vt)
        n_new = a_l[..., None] * n + jnp.einsum('bhs,bhsd->bhd', w_s, kt)
        return (C_new, n_new, m_new), h

    init = (jnp.zeros((bsz, nh, d, d), f32), jnp.zeros((bsz, nh, d), f32), jnp.zeros((bsz, nh), f32))
    _, hs = lax.scan(step, init, (qc, kc, vc, ic, lfc))
    return jnp.moveaxis(hs, (0, 2), (1, 3)).reshape(bsz, s, nh, d)


def peer(xn, wq, sub_keys, u_tab, v_tab):
    bsz, s, dm = xn.shape
    q = (xn @ wq).reshape(bsz, s, PEER_HEADS, 2, PEER_HALF)
    scores = jnp.einsum('bshpc,hpnc->bshpn', q, sub_keys).astype(jnp.float32)
    s_top, i_top = lax.top_k(scores, PEER_TOPK)
    cand_s = s_top[..., 0, :, None] + s_top[..., 1, None, :]
    cand_i = i_top[..., 0, :, None] * PEER_N_KEYS + i_top[..., 1, None, :]
    cand_s = cand_s.reshape(bsz, s, PEER_HEADS, PEER_TOPK * PEER_TOPK)
    cand_i = cand_i.reshape(bsz, s, PEER_HEADS, PEER_TOPK * PEER_TOPK)
    best_s, pos = lax.top_k(cand_s, PEER_TOPK)
    idx = jnp.take_along_axis(cand_i, pos, axis=-1)
    gate = jax.nn.softmax(best_s, axis=-1).astype(xn.dtype)
    nb = (bsz * s) // PEER_TOKEN_BLOCK
    xb = xn.reshape(nb, PEER_TOKEN_BLOCK, dm)
    ib = idx.reshape(nb, PEER_TOKEN_BLOCK, PEER_HEADS, PEER_TOPK)
    gb = gate.reshape(nb, PEER_TOKEN_BLOCK, PEER_HEADS, PEER_TOPK)

    def block(args):
        xt, it, gt = args
        u = u_tab[it]
        a = jnp.einsum('thkd,td->thk', u, xt)
        w = jax.nn.gelu(a, approximate=False) * gt
        return jnp.einsum('thk,thkd->td', w, v_tab[it])

    out = lax.map(block, (xb, ib, gb))
    return out.reshape(bsz, s, dm)


def setup_inputs(seed: int = 0) -> dict:
    key = jax.random.key(seed)
    ks = jax.random.split(key, 17)
    nrm = jax.random.normal
    f32 = jnp.float32
    x = nrm(ks[0], (BATCH, SEQ, D_MODEL), f32)
    p = nrm(ks[1], (DEPTH, BATCH, SEQ, PLE_DIM), f32)
    norm_mix_g = 1.0 + 0.02 * nrm(ks[2], (DEPTH, D_MODEL), f32)
    w_in = nrm(ks[3], (DEPTH, D_MODEL, IN_COLS), f32) * D_MODEL ** -0.5
    conv_w = nrm(ks[4], (DEPTH, CONV_K, CONV_DIM), f32) * CONV_K ** -0.5
    gate_base = jnp.concatenate([jnp.full((DEPTH, MLSTM_HEADS), -2.0, f32),
                                 jnp.broadcast_to(jnp.linspace(3.0, 6.0, MLSTM_HEADS, dtype=f32), (DEPTH, MLSTM_HEADS))], axis=-1)
    mlstm_gate_b = gate_base + 0.1 * nrm(ks[5], (DEPTH, 2 * MLSTM_HEADS), f32)
    mlstm_norm_g = 1.0 + 0.02 * nrm(ks[6], (DEPTH, MLSTM_DIM), f32)
    w_out = nrm(ks[7], (DEPTH, MIX_DIM, D_MODEL), f32) * MIX_DIM ** -0.5
    norm_ffn_g = 1.0 + 0.02 * nrm(ks[8], (DEPTH, D_MODEL), f32)
    peer_wq = nrm(ks[9], (DEPTH, D_MODEL, PEER_HEADS * PEER_KEY_DIM), f32) * D_MODEL ** -0.5
    peer_keys = nrm(ks[10], (DEPTH, PEER_HEADS, 2, PEER_N_KEYS, PEER_HALF), f32) * PEER_HALF ** -0.5
    peer_u = nrm(ks[11], (DEPTH, PEER_N_EXPERTS, D_MODEL), f32) * D_MODEL ** -0.5
    peer_v = nrm(ks[12], (DEPTH, PEER_N_EXPERTS, D_MODEL), f32) * D_MODEL ** -0.5
    norm_ple_g = 1.0 + 0.02 * nrm(ks[13], (DEPTH, D_MODEL), f32)
    ple_w_gate = nrm(ks[14], (DEPTH, D_MODEL, D_MODEL), f32) * D_MODEL ** -0.5
    ple_w_proj = nrm(ks[15], (DEPTH, PLE_DIM, D_MODEL), f32) * PLE_DIM ** -0.5
    norm_final_g = 1.0 + 0.02 * nrm(ks[16], (D_MODEL,), f32)
    return {"x": x, "p": p, "norm_mix_g": norm_mix_g, "w_in": w_in, "conv_w": conv_w,
            "mlstm_gate_b": mlstm_gate_b, "mlstm_norm_g": mlstm_norm_g, "w_out": w_out,
            "norm_ffn_g": norm_ffn_g, "peer_wq": peer_wq, "peer_keys": peer_keys,
            "peer_u": peer_u, "peer_v": peer_v, "norm_ple_g": norm_ple_g,
            "ple_w_gate": ple_w_gate, "ple_w_proj": ple_w_proj, "norm_final_g": norm_final_g}


def reference(x, p, norm_mix_g, w_in, conv_w, mlstm_gate_b, mlstm_norm_g, w_out,
              norm_ffn_g, peer_wq, peer_keys, peer_u, peer_v, norm_ple_g,
              ple_w_gate, ple_w_proj, norm_final_g):
    bsz, s, _ = x.shape
    c, m = CONV_DIM, MLSTM_DIM
    splits = [c, 2 * c, 3 * c, 3 * c + m, 3 * c + 2 * m, 3 * c + 3 * m, 3 * c + 4 * m]
    h = x
    for i in range(DEPTH):
        xn = rmsnorm(h, norm_mix_g[i])
        proj = xn @ w_in[i]
        bc, cc, uc, q, k, v, o, gates = jnp.split(proj, splits, axis=-1)
        y_conv = short_conv(bc, cc, uc, conv_w[i])
        gates = gates.astype(jnp.float32) + mlstm_gate_b[i].astype(jnp.float32)
        i_pre, f_pre = gates[..., :MLSTM_HEADS], gates[..., MLSTM_HEADS:]
        hd = (bsz, s, MLSTM_HEADS, MLSTM_HEAD_DIM)
        hm = mlstm_chunkwise(q.reshape(hd), k.reshape(hd), v.reshape(hd), i_pre, f_pre)
        hm = rmsnorm(hm, mlstm_norm_g[i].reshape(MLSTM_HEADS, MLSTM_HEAD_DIM).astype(jnp.float32))
        y_m = jax.nn.sigmoid(o) * hm.reshape(bsz, s, MLSTM_DIM).astype(x.dtype)
        h = h + jnp.concatenate([y_conv, y_m], axis=-1) @ w_out[i]
        h = h + peer(rmsnorm(h, norm_ffn_g[i]), peer_wq[i], peer_keys[i], peer_u[i], peer_v[i])
        g = jax.nn.sigmoid(rmsnorm(h, norm_ple_g[i]) @ ple_w_gate[i])
        h = h + g * (p[i] @ ple_w_proj[i])
    return rmsnorm(h, norm_final_g)
```

```python
import functools
import math

import jax
import jax.numpy as jnp
from jax import lax
from jax.experimental import pallas as pl
from jax.experimental.pallas import tpu as pltpu

F32 = jnp.float32
BF16 = jnp.bfloat16

D_MODEL = 2048
CONV_DIM = 1024
MLSTM_HEADS = 4
MLSTM_HEAD_DIM = 256
MLSTM_DIM = MLSTM_HEADS * MLSTM_HEAD_DIM
MLSTM_CHUNK = 128
MAIN_COLS = 3 * CONV_DIM + 4 * MLSTM_DIM
N_GATES = 2 * MLSTM_HEADS
PEER_HEADS = 8
PEER_N_KEYS = 128
PEER_HALF = 128
PEER_TOPK = 16
EPS = 1e-6

LANES = 128
VMEM_LIMIT = 60 * 1024 * 1024


def _params(semantics):
    return pltpu.CompilerParams(dimension_semantics=semantics, vmem_limit_bytes=VMEM_LIMIT)


def _rms(x, g):
    r = lax.rsqrt(jnp.mean(x * x, axis=-1, keepdims=True) + EPS)
    return (x * r) * g


def _dot(a, b):
    return jnp.dot(a, b, preferred_element_type=F32)


def _dot_nt(a, b):
    return lax.dot_general(a, b, (((1,), (1,)), ((), ())), preferred_element_type=F32)


def _rms_proj_kernel(x_ref, g_ref, w_ref, wg_ref, o_ref, og_ref, xn_ref):
    @pl.when(pl.program_id(1) == 0)
    def _():
        xn = _rms(x_ref[...], g_ref[...]).astype(BF16)
        xn_ref[...] = xn
        og_ref[...] = _dot(xn, wg_ref[...])

    o_ref[...] = _dot(xn_ref[...], w_ref[...])


def _rms_proj(x2, g, w_main, w_gate, tm=512, tn=1024):
    t, d = x2.shape
    n = w_main.shape[1]
    return pl.pallas_call(
        _rms_proj_kernel,
        grid=(t // tm, n // tn),
        in_specs=[
            pl.BlockSpec((tm, d), lambda i, j: (i, 0)),
            pl.BlockSpec((1, d), lambda i, j: (0, 0)),
            pl.BlockSpec((d, tn), lambda i, j: (0, j)),
            pl.BlockSpec((d, LANES), lambda i, j: (0, 0)),
        ],
        out_specs=[
            pl.BlockSpec((tm, tn), lambda i, j: (i, j)),
            pl.BlockSpec((tm, LANES), lambda i, j: (i, 0)),
        ],
        out_shape=[
            jax.ShapeDtypeStruct((t, n), F32),
            jax.ShapeDtypeStruct((t, LANES), F32),
        ],
        scratch_shapes=[pltpu.VMEM((tm, d), BF16)],
        compiler_params=_params(("parallel", "arbitrary")),
        name="rms_proj",
    )(x2, g, w_main, w_gate)


def _log_sigmoid(x):
    return jnp.minimum(x, 0.0) - jnp.log1p(jnp.exp(-jnp.abs(x)))


def _mix_kernel(bc_ref, cc_ref, uc_ref, q_ref, k_ref, v_ref, o_ref, gc_ref, gr_ref,
                convw_ref, gbc_ref, gbr_ref, ng_ref, y_ref, caug_ref, m_ref, zc_ref):
    L = MLSTM_CHUNK
    hd = MLSTM_HEAD_DIM

    @pl.when(pl.program_id(1) == 0)
    def _():
        caug_ref[...] = jnp.zeros_like(caug_ref)
        m_ref[...] = jnp.zeros_like(m_ref)
        zc_ref[...] = jnp.zeros_like(zc_ref)

    z = cc_ref[...] * uc_ref[...]
    row = lax.broadcasted_iota(jnp.int32, z.shape, 0)
    prev = zc_ref[...]
    p1 = prev[7:8, :]
    p2 = prev[6:7, :]
    z1 = jnp.where(row == 0, p1, pltpu.roll(z, 1, 0))
    z2 = jnp.where(row == 0, p2, jnp.where(row == 1, p1, pltpu.roll(z, 2, 0)))
    cw = convw_ref[...]
    y_conv = bc_ref[...] * (cw[0:1, :] * z2 + cw[1:2, :] * z1 + cw[2:3, :] * z)
    zc_ref[...] = z[L - 8:, :]
    y_ref[:, :CONV_DIM] = y_conv.astype(y_ref.dtype)

    gcol = gc_ref[...] + gbc_ref[...]
    grow = gr_ref[...] + gbr_ref[...]
    ri = lax.broadcasted_iota(jnp.int32, (L, L), 0)
    ci = lax.broadcasted_iota(jnp.int32, (L, L), 1)
    causal = ri >= ci
    tril = causal.astype(F32)
    triu = (ri <= ci).astype(F32)
    hi = lax.Precision.HIGHEST
    b_col_all = jnp.dot(tril, _log_sigmoid(gcol), precision=hi, preferred_element_type=F32)
    b_row_all = jnp.dot(_log_sigmoid(grow), triu, precision=hi, preferred_element_type=F32)
    lane = lax.broadcasted_iota(jnp.int32, (L, LANES), 1)
    ones_col = (lane == 0).astype(BF16)
    scale = hd ** -0.5

    for h in range(MLSTM_HEADS):
        cs = slice(h * hd, (h + 1) * hd)
        qh = q_ref[:, cs].astype(BF16)
        kh = k_ref[:, cs]
        vaug = jnp.concatenate([v_ref[:, cs].astype(BF16), ones_col], axis=1)
        i_row = grow[h:h + 1, :]
        i_col = gcol[:, h:h + 1]
        b_row = b_row_all[MLSTM_HEADS + h:MLSTM_HEADS + h + 1, :]
        b_col = b_col_all[:, MLSTM_HEADS + h:MLSTM_HEADS + h + 1]
        m_prev = m_ref[h:h + 1, 0:1]

        dmat = jnp.where(causal, b_col - b_row + i_row, -jnp.inf)
        inter = b_col + m_prev
        m_t = jnp.maximum(inter, jnp.max(dmat, axis=-1, keepdims=True))
        pw = jnp.exp(dmat - m_t)
        sc = (_dot_nt(qh, kh.astype(BF16)) * scale) * pw
        a = jnp.exp(inter - m_t)
        caug = caug_ref[h]
        nd = a * _dot(qh, caug.astype(BF16)) + _dot(sc.astype(BF16), vaug)
        num = nd[:, :hd]
        den = nd[:, hd:hd + 1]
        hh = num / jnp.maximum(jnp.abs(den), jnp.exp(-m_t))
        hn = _rms(hh, ng_ref[:, cs])
        y_ref[:, CONV_DIM + h * hd:CONV_DIM + (h + 1) * hd] = (
            jax.nn.sigmoid(o_ref[:, cs]) * hn).astype(y_ref.dtype)

        m_new = m_t[L - 1:L, :]
        b_last = b_col[L - 1:L, :]
        w_s = jnp.exp(b_last - b_col + i_col - m_new)
        a_l = jnp.exp(b_last + m_prev - m_new)
        kw = (kh * (w_s * scale)).astype(BF16)
        caug_ref[h] = a_l * caug + _dot(kw.T, vaug)
        m_ref[h:h + 1, :] = jnp.broadcast_to(m_new, (1, LANES))


def _mix(proj, gates_col, gates_row, conv_w, gate_b_col, gate_b_row, norm_g, batch, seq):
    L = MLSTM_CHUNK
    nc = seq // L
    t = batch * seq
    blk = CONV_DIM

    def slab(k):
        return pl.BlockSpec((L, blk), lambda b, c, k=k: (b * nc + c, k))

    return pl.pallas_call(
        _mix_kernel,
        grid=(batch, nc),
        in_specs=[slab(0), slab(1), slab(2), slab(3), slab(4), slab(5), slab(6),
                  pl.BlockSpec((L, LANES), lambda b, c: (b * nc + c, 0)),
                  pl.BlockSpec((N_GATES, L), lambda b, c: (0, b * nc + c)),
                  pl.BlockSpec((3, CONV_DIM), lambda b, c: (0, 0)),
                  pl.BlockSpec((1, LANES), lambda b, c: (0, 0)),
                  pl.BlockSpec((N_GATES, 1), lambda b, c: (0, 0)),
                  pl.BlockSpec((1, MLSTM_DIM), lambda b, c: (0, 0))],
        out_specs=pl.BlockSpec((L, CONV_DIM + MLSTM_DIM), lambda b, c: (b * nc + c, 0)),
        out_shape=jax.ShapeDtypeStruct((t, CONV_DIM + MLSTM_DIM), BF16),
        scratch_shapes=[pltpu.VMEM((MLSTM_HEADS, MLSTM_HEAD_DIM, MLSTM_HEAD_DIM + LANES), F32),
                        pltpu.VMEM((8, LANES), F32),
                        pltpu.VMEM((8, CONV_DIM), F32)],
        compiler_params=_params(("parallel", "arbitrary")),
        name="mix",
    )(proj, proj, proj, proj, proj, proj, proj, gates_col, gates_row, conv_w,
      gate_b_col, gate_b_row, norm_g)


def _matmul_res_kernel(a_ref, w_ref, r_ref, o_ref):
    o_ref[...] = r_ref[...] + _dot(a_ref[...], w_ref[...])


def _matmul_res(a, w, res, tm=512):
    t, k = a.shape
    n = w.shape[1]
    return pl.pallas_call(
        _matmul_res_kernel,
        grid=(t // tm,),
        in_specs=[pl.BlockSpec((tm, k), lambda i: (i, 0)),
                  pl.BlockSpec((k, n), lambda i: (0, 0)),
                  pl.BlockSpec((tm, n), lambda i: (i, 0))],
        out_specs=pl.BlockSpec((tm, n), lambda i: (i, 0)),
        out_shape=jax.ShapeDtypeStruct((t, n), F32),
        compiler_params=_params(("parallel",)),
        name="matmul_res",
    )(a, w, res)


def _topk_rank(s, k):
    n = s.shape[0]
    iota = lax.broadcasted_iota(jnp.int32, s.shape, 0).astype(F32)
    rank = jnp.full(s.shape, float(k), F32)
    vals = []
    for r in range(k):
        mx = jnp.max(s, axis=0, keepdims=True)
        idx = jnp.min(jnp.where(s == mx, iota, float(n)), axis=0, keepdims=True)
        sel = iota == idx
        rank = jnp.where(sel, float(r), rank)
        s = jnp.where(sel, -jnp.inf, s)
        vals.append(mx)
    return rank, vals


def _peer_select_kernel(h_ref, g_ref, wq_ref, keys_ref, xn_ref, r2_ref, bj_ref, cnt_ref, ai_ref,
                        q_ref):
    hidx = pl.program_id(1)
    K = PEER_TOPK

    @pl.when(hidx == 0)
    def _():
        xn = _rms(h_ref[...], g_ref[...]).astype(BF16)
        xn_ref[...] = xn
        q = _dot(xn, wq_ref[...]).astype(BF16)
        for g in range(2 * PEER_HEADS):
            q_ref[g] = q[:, g * PEER_HALF:(g + 1) * PEER_HALF]

    s1 = _dot_nt(keys_ref[0, 0], q_ref[2 * hidx])
    s2 = _dot_nt(keys_ref[0, 1], q_ref[2 * hidx + 1])
    rank1, a_vals = _topk_rank(s1, K)
    rank2, b_vals = _topk_rank(s2, K)
    bmat = jnp.concatenate(b_vals, axis=0)
    cmat = jnp.concatenate([a_vals[x] + bmat for x in range(K)], axis=0)
    rank_c, c_vals = _topk_rank(cmat, K)
    chosen = (rank_c < float(K)).astype(F32)
    z = jnp.zeros_like(c_vals[0])
    for v in c_vals:
        z = z + jnp.exp(v - c_vals[0])
    cnt_i = jnp.zeros_like(s1)
    for x in range(K):
        cnt_x = jnp.sum(chosen[x * K:(x + 1) * K, :], axis=0, keepdims=True)
        cnt_i = cnt_i + jnp.where(rank1 == float(x), cnt_x, 0.0)
    r2_ref[0] = rank2
    cnt_ref[0] = cnt_i
    ai_ref[0] = jnp.exp(s1 - a_vals[0]) / z
    bj_ref[0] = jnp.exp(s2 - b_vals[0])


def _peer_select(h1, g, wq, keys, tt=256):
    t, d = h1.shape
    nk = PEER_N_KEYS
    stat = jax.ShapeDtypeStruct((PEER_HEADS, nk, t), F32)
    stat_spec = pl.BlockSpec((1, nk, tt), lambda i, h: (h, 0, i))
    return pl.pallas_call(
        _peer_select_kernel,
        grid=(t // tt, PEER_HEADS),
        in_specs=[pl.BlockSpec((tt, d), lambda i, h: (i, 0)),
                  pl.BlockSpec((1, d), lambda i, h: (0, 0)),
                  pl.BlockSpec((d, wq.shape[1]), lambda i, h: (0, 0)),
                  pl.BlockSpec((1, 2, nk, PEER_HALF), lambda i, h: (h, 0, 0, 0))],
        out_specs=[pl.BlockSpec((tt, d), lambda i, h: (i, 0)),
                   stat_spec, stat_spec, stat_spec, stat_spec],
        out_shape=[jax.ShapeDtypeStruct((t, d), BF16), stat, stat, stat, stat],
        scratch_shapes=[pltpu.VMEM((2 * PEER_HEADS, tt, PEER_HALF), BF16)],
        compiler_params=_params(("parallel", "arbitrary")),
        name="peer_select",
    )(h1, g, wq, keys)


def _gelu(a):
    return 0.5 * a * (1.0 + lax.erf(a * (1.0 / math.sqrt(2.0))))


def _peer_dense_kernel(xn_ref, u_ref, v_ref, r2_ref, bj_ref, cnt_ref, ai_ref, o_ref, m_ref):
    @pl.when(pl.program_id(1) == 0)
    def _():
        o_ref[...] = jnp.zeros_like(o_ref)

    et = u_ref.shape[0]
    for il in range(et // PEER_N_KEYS):
        acc = None
        for h in range(PEER_HEADS):
            hit = r2_ref[h] < cnt_ref[il, h:h + 1, :]
            term = jnp.where(hit, ai_ref[il, h:h + 1, :] * bj_ref[h], 0.0)
            acc = term if acc is None else acc + term
        m_ref[:, il * PEER_N_KEYS:(il + 1) * PEER_N_KEYS] = acc.T
    a = _dot_nt(xn_ref[...], u_ref[...])
    w = (_gelu(a) * m_ref[...]).astype(BF16)
    o_ref[...] += _dot(w, v_ref[...])


def _peer_dense(xn, u, v, r2, bj, cnt, ai, tt=1024, et=512):
    t, d = xn.shape
    ne = u.shape[0]
    nk = PEER_N_KEYS
    per_j = pl.BlockSpec((PEER_HEADS, nk, tt), lambda i, e: (0, 0, i))
    per_i = pl.BlockSpec((et // nk, PEER_HEADS, tt), lambda i, e: (e, 0, i))
    return pl.pallas_call(
        _peer_dense_kernel,
        grid=(t // tt, ne // et),
        in_specs=[pl.BlockSpec((tt, d), lambda i, e: (i, 0)),
                  pl.BlockSpec((et, d), lambda i, e: (e, 0)),
                  pl.BlockSpec((et, d), lambda i, e: (e, 0)),
                  per_j, per_j, per_i, per_i],
        out_specs=pl.BlockSpec((tt, d), lambda i, e: (i, 0)),
        out_shape=jax.ShapeDtypeStruct((t, d), F32),
        scratch_shapes=[pltpu.VMEM((tt, et), F32)],
        compiler_params=_params(("parallel", "arbitrary")),
        name="peer_dense",
    )(xn, u, v, r2, bj, cnt, ai)


def _final_kernel(h1_ref, peer_ref, p_ref, g3_ref, wg_ref, wp_ref, gf_ref, o_ref, *, last):
    h2 = h1_ref[...] + peer_ref[...]
    xn = _rms(h2, g3_ref[...]).astype(BF16)
    gate = jax.nn.sigmoid(_dot(xn, wg_ref[...]))
    h3 = h2 + gate * _dot(p_ref[...].astype(BF16), wp_ref[...])
    o_ref[...] = _rms(h3, gf_ref[...]) if last else h3


def _final(h1, peer, p2, g3, wg, wp, gf, last, tm=256):
    t, d = h1.shape
    pd = p2.shape[1]
    row = pl.BlockSpec((tm, d), lambda i: (i, 0))
    vec = pl.BlockSpec((1, d), lambda i: (0, 0))
    return pl.pallas_call(
        functools.partial(_final_kernel, last=last),
        grid=(t // tm,),
        in_specs=[row, row, pl.BlockSpec((tm, pd), lambda i: (i, 0)), vec,
                  pl.BlockSpec((d, d), lambda i: (0, 0)),
                  pl.BlockSpec((pd, d), lambda i: (0, 0)), vec],
        out_specs=row,
        out_shape=jax.ShapeDtypeStruct((t, d), F32),
        compiler_params=_params(("parallel",)),
        name="ple_final",
    )(h1, peer, p2, g3, wg, wp, gf)


def kernel(x, p, norm_mix_g, w_in, conv_w, mlstm_gate_b, mlstm_norm_g, w_out, norm_ffn_g,
           peer_wq, peer_keys, peer_u, peer_v, norm_ple_g, ple_w_gate, ple_w_proj, norm_final_g):
    bsz, seq, d = x.shape
    depth = w_in.shape[0]
    t = bsz * seq
    h = x.reshape(t, d)
    for i in range(depth):
        w_main = w_in[i, :, :MAIN_COLS].astype(BF16)
        w_gates = jnp.pad(w_in[i, :, MAIN_COLS:], ((0, 0), (0, LANES - N_GATES))).astype(BF16)
        proj, gates = _rms_proj(h, norm_mix_g[i].reshape(1, d), w_main, w_gates)
        gate_b = mlstm_gate_b[i].astype(F32)
        mix = _mix(proj, gates, gates[:, :N_GATES].T, conv_w[i],
                   jnp.pad(gate_b, (0, LANES - N_GATES)).reshape(1, LANES),
                   gate_b.reshape(N_GATES, 1), mlstm_norm_g[i].reshape(1, MLSTM_DIM), bsz, seq)
        h1 = _matmul_res(mix, w_out[i].astype(BF16), h)
        xn, r2, bj, cnt, ai = _peer_select(h1, norm_ffn_g[i].reshape(1, d),
                                           peer_wq[i].astype(BF16), peer_keys[i].astype(BF16))
        peer = _peer_dense(xn, peer_u[i].astype(BF16), peer_v[i].astype(BF16), r2, bj,
                           jnp.transpose(cnt, (1, 0, 2)), jnp.transpose(ai, (1, 0, 2)))
        h = _final(h1, peer, p[i].reshape(t, -1), norm_ple_g[i].reshape(1, d),
                   ple_w_gate[i].astype(BF16), ple_w_proj[i].astype(BF16),
                   norm_final_g.reshape(1, d), i == depth - 1)
    return h.reshape(bsz, seq, d)
```

```python
import functools
import math

import jax
import jax.numpy as jnp
from jax import lax
from jax.experimental import pallas as pl
from jax.experimental.pallas import tpu as pltpu

F32 = jnp.float32
BF16 = jnp.bfloat16

D_MODEL = 2048
CONV_DIM = 1024
MLSTM_HEADS = 4
MLSTM_HEAD_DIM = 256
MLSTM_DIM = MLSTM_HEADS * MLSTM_HEAD_DIM
MLSTM_CHUNK = 128
MAIN_COLS = 3 * CONV_DIM + 4 * MLSTM_DIM
N_GATES = 2 * MLSTM_HEADS
PEER_HEADS = 8
PEER_N_KEYS = 128
PEER_HALF = 128
PEER_TOPK = 16
EPS = 1e-6

LANES = 128
VMEM_LIMIT = 60 * 1024 * 1024


def _params(semantics):
    return pltpu.CompilerParams(dimension_semantics=semantics, vmem_limit_bytes=VMEM_LIMIT)


def _rms(x, g):
    r = lax.rsqrt(jnp.mean(x * x, axis=-1, keepdims=True) + EPS)
    return (x * r) * g


def _dot(a, b):
    return jnp.dot(a, b, preferred_element_type=F32)


def _dot_nt(a, b):
    return lax.dot_general(a, b, (((1,), (1,)), ((), ())), preferred_element_type=F32)


def _rms_proj_kernel(x_ref, g_ref, w_ref, wg_ref, o_ref, og_ref, xn_ref):
    @pl.when(pl.program_id(1) == 0)
    def _():
        xn = _rms(x_ref[...], g_ref[...]).astype(BF16)
        xn_ref[...] = xn
        og_ref[...] = _dot(xn, wg_ref[...])

    o_ref[...] = _dot(xn_ref[...], w_ref[...])


def _rms_proj(x2, g, w_main, w_gate, tm=512, tn=1024):
    t, d = x2.shape
    n = w_main.shape[1]
    return pl.pallas_call(
        _rms_proj_kernel,
        grid=(t // tm, n // tn),
        in_specs=[
            pl.BlockSpec((tm, d), lambda i, j: (i, 0)),
            pl.BlockSpec((1, d), lambda i, j: (0, 0)),
            pl.BlockSpec((d, tn), lambda i, j: (0, j)),
            pl.BlockSpec((d, LANES), lambda i, j: (0, 0)),
        ],
        out_specs=[
            pl.BlockSpec((tm, tn), lambda i, j: (i, j)),
            pl.BlockSpec((tm, LANES), lambda i, j: (i, 0)),
        ],
        out_shape=[
            jax.ShapeDtypeStruct((t, n), F32),
            jax.ShapeDtypeStruct((t, LANES), F32),
        ],
        scratch_shapes=[pltpu.VMEM((tm, d), BF16)],
        compiler_params=_params(("parallel", "arbitrary")),
        name="rms_proj",
    )(x2, g, w_main, w_gate)


def _log_sigmoid(x):
    return jnp.minimum(x, 0.0) - jnp.log1p(jnp.exp(-jnp.abs(x)))


def _mix_kernel(bc_ref, cc_ref, uc_ref, q_ref, k_ref, v_ref, o_ref, gc_ref, gr_ref,
                convw_ref, gbc_ref, gbr_ref, ng_ref, y_ref, caug_ref, m_ref, zc_ref):
    L = MLSTM_CHUNK
    hd = MLSTM_HEAD_DIM

    @pl.when(pl.program_id(1) == 0)
    def _():
        caug_ref[...] = jnp.zeros_like(caug_ref)
        m_ref[...] = jnp.zeros_like(m_ref)
        zc_ref[...] = jnp.zeros_like(zc_ref)

    z = cc_ref[...] * uc_ref[...]
    row = lax.broadcasted_iota(jnp.int32, z.shape, 0)
    prev = zc_ref[...]
    p1 = prev[7:8, :]
    p2 = prev[6:7, :]
    z1 = jnp.where(row == 0, p1, pltpu.roll(z, 1, 0))
    z2 = jnp.where(row == 0, p2, jnp.where(row == 1, p1, pltpu.roll(z, 2, 0)))
    cw = convw_ref[...]
    y_conv = bc_ref[...] * (cw[0:1, :] * z2 + cw[1:2, :] * z1 + cw[2:3, :] * z)
    zc_ref[...] = z[L - 8:, :]
    y_ref[:, :CONV_DIM] = y_conv.astype(y_ref.dtype)

    gcol = gc_ref[...] + gbc_ref[...]
    grow = gr_ref[...] + gbr_ref[...]
    ri = lax.broadcasted_iota(jnp.int32, (L, L), 0)
    ci = lax.broadcasted_iota(jnp.int32, (L, L), 1)
    causal = ri >= ci
    tril = causal.astype(F32)
    triu = (ri <= ci).astype(F32)
    hi = lax.Precision.HIGHEST
    b_col_all = jnp.dot(tril, _log_sigmoid(gcol), precision=hi, preferred_element_type=F32)
    b_row_all = jnp.dot(_log_sigmoid(grow), triu, precision=hi, preferred_element_type=F32)
    lane = lax.broadcasted_iota(jnp.int32, (L, LANES), 1)
    ones_col = (lane == 0).astype(BF16)
    scale = hd ** -0.5

    for h in range(MLSTM_HEADS):
        cs = slice(h * hd, (h + 1) * hd)
        qh = q_ref[:, cs].astype(BF16)
        kh = k_ref[:, cs]
        vaug = jnp.concatenate([v_ref[:, cs].astype(BF16), ones_col], axis=1)
        i_row = grow[h:h + 1, :]
        i_col = gcol[:, h:h + 1]
        b_row = b_row_all[MLSTM_HEADS + h:MLSTM_HEADS + h + 1, :]
        b_col = b_col_all[:, MLSTM_HEADS + h:MLSTM_HEADS + h + 1]
        m_prev = m_ref[h:h + 1, 0:1]

        dmat = jnp.where(causal, b_col - b_row + i_row, -jnp.inf)
        inter = b_col + m_prev
        m_t = jnp.maximum(inter, jnp.max(dmat, axis=-1, keepdims=True))
        pw = jnp.exp(dmat - m_t)
        sc = (_dot_nt(qh, kh.astype(BF16)) * scale) * pw
        a = jnp.exp(inter - m_t)
        caug = caug_ref[h]
        nd = a * _dot(qh, caug.astype(BF16)) + _dot(sc.astype(BF16), vaug)
        num = nd[:, :hd]
        den = nd[:, hd:hd + 1]
        hh = num / jnp.maximum(jnp.abs(den), jnp.exp(-m_t))
        hn = _rms(hh, ng_ref[:, cs])
        y_ref[:, CONV_DIM + h * hd:CONV_DIM + (h + 1) * hd] = (
            jax.nn.sigmoid(o_ref[:, cs]) * hn).astype(y_ref.dtype)

        m_new = m_t[L - 1:L, :]
        b_last = b_col[L - 1:L, :]
        w_s = jnp.exp(b_last - b_col + i_col - m_new)
        a_l = jnp.exp(b_last + m_prev - m_new)
        kw = (kh * (w_s * scale)).astype(BF16)
        caug_ref[h] = a_l * caug + _dot(kw.T, vaug)
        m_ref[h:h + 1, :] = jnp.broadcast_to(m_new, (1, LANES))


def _mix(proj, gates_col, gates_row, conv_w, gate_b_col, gate_b_row, norm_g, batch, seq):
    L = MLSTM_CHUNK
    nc = seq // L
    t = batch * seq
    blk = CONV_DIM

    def slab(k):
        return pl.BlockSpec((L, blk), lambda b, c, k=k: (b * nc + c, k))

    return pl.pallas_call(
        _mix_kernel,
        grid=(batch, nc),
        in_specs=[slab(0), slab(1), slab(2), slab(3), slab(4), slab(5), slab(6),
                  pl.BlockSpec((L, LANES), lambda b, c: (b * nc + c, 0)),
                  pl.BlockSpec((N_GATES, L), lambda b, c: (0, b * nc + c)),
                  pl.BlockSpec((3, CONV_DIM), lambda b, c: (0, 0)),
                  pl.BlockSpec((1, LANES), lambda b, c: (0, 0)),
                  pl.BlockSpec((N_GATES, 1), lambda b, c: (0, 0)),
                  pl.BlockSpec((1, MLSTM_DIM), lambda b, c: (0, 0))],
        out_specs=pl.BlockSpec((L, CONV_DIM + MLSTM_DIM), lambda b, c: (b * nc + c, 0)),
        out_shape=jax.ShapeDtypeStruct((t, CONV_DIM + MLSTM_DIM), BF16),
        scratch_shapes=[pltpu.VMEM((MLSTM_HEADS, MLSTM_HEAD_DIM, MLSTM_HEAD_DIM + LANES), F32),
                        pltpu.VMEM((8, LANES), F32),
                        pltpu.VMEM((8, CONV_DIM), F32)],
        compiler_params=_params(("parallel", "arbitrary")),
        name="mix",
    )(proj, proj, proj, proj, proj, proj, proj, gates_col, gates_row, conv_w,
      gate_b_col, gate_b_row, norm_g)


def _matmul_res_kernel(a_ref, w_ref, r_ref, o_ref):
    o_ref[...] = r_ref[...] + _dot(a_ref[...], w_ref[...])


def _matmul_res(a, w, res, tm=512):
    t, k = a.shape
    n = w.shape[1]
    return pl.pallas_call(
        _matmul_res_kernel,
        grid=(t // tm,),
        in_specs=[pl.BlockSpec((tm, k), lambda i: (i, 0)),
                  pl.BlockSpec((k, n), lambda i: (0, 0)),
                  pl.BlockSpec((tm, n), lambda i: (i, 0))],
        out_specs=pl.BlockSpec((tm, n), lambda i: (i, 0)),
        out_shape=jax.ShapeDtypeStruct((t, n), F32),
        compiler_params=_params(("parallel",)),
        name="matmul_res",
    )(a, w, res)


def _topk_rank(s, k):
    n = s.shape[0]
    iota = lax.broadcasted_iota(jnp.int32, s.shape, 0).astype(F32)
    rank = jnp.full(s.shape, float(k), F32)
    vals = []
    for r in range(k):
        mx = jnp.max(s, axis=0, keepdims=True)
        idx = jnp.min(jnp.where(s == mx, iota, float(n)), axis=0, keepdims=True)
        sel = iota == idx
        rank = jnp.where(sel, float(r), rank)
        s = jnp.where(sel, -jnp.inf, s)
        vals.append(mx)
    return rank, vals


def _route_exact(s1, s2):
    K = PEER_TOPK
    rank1, a_vals = _topk_rank(s1, K)
    rank2, b_vals = _topk_rank(s2, K)
    bmat = jnp.concatenate(b_vals, axis=0)
    cmat = jnp.concatenate([a_vals[x] + bmat for x in range(K)], axis=0)
    rank_c, c_vals = _topk_rank(cmat, K)
    chosen = (rank_c < float(K)).astype(F32)
    z = jnp.zeros_like(c_vals[0])
    for v in c_vals:
        z = z + jnp.exp(v - c_vals[0])
    cnt_i = jnp.zeros_like(s1)
    for x in range(K):
        cnt_x = jnp.sum(chosen[x * K:(x + 1) * K, :], axis=0, keepdims=True)
        cnt_i = cnt_i + jnp.where(rank1 == float(x), cnt_x, 0.0)
    return rank2, cnt_i, jnp.exp(s1 - a_vals[0]) / z, jnp.exp(s2 - b_vals[0])


def _sort16_network():
    def merge(lo, hi, r):
        step = r * 2
        if step < hi - lo:
            yield from merge(lo, hi, step)
            yield from merge(lo + r, hi, step)
            yield from [(i, i + r) for i in range(lo + r, hi - r, step)]
        else:
            yield (lo, lo + r)

    def sort(lo, hi):
        if hi - lo >= 1:
            mid = lo + (hi - lo) // 2
            yield from sort(lo, mid)
            yield from sort(mid + 1, hi)
            yield from merge(lo, hi, 1)

    return tuple(sort(0, 15))


_SORT16 = _sort16_network()
SUBLANES = 8


def _top16_sorted(tiles):
    v = list(tiles) + [None] * (16 - len(tiles))
    for i, j in _SORT16:
        a, b = v[i], v[j]
        if b is None:
            continue
        if a is None:
            v[i], v[j] = b, None
        else:
            v[i], v[j] = jnp.maximum(a, b), jnp.minimum(a, b)
    neg = jnp.full(tiles[0].shape, -jnp.inf, F32)
    v = [neg if x is None else x for x in v]
    for shift in (4, 2, 1):
        r = [pltpu.roll(x, shift, 0) for x in v]
        v = [jnp.maximum(v[k], r[15 - k]) for k in range(16)]
        d = 8
        while d >= 1:
            for k in range(16):
                if (k & d) == 0:
                    v[k], v[k + d] = jnp.maximum(v[k], v[k + d]), jnp.minimum(v[k], v[k + d])
            d //= 2
    return v


def _route_sorted(s1, s2):
    K = PEER_TOPK
    nk, n = s1.shape
    nt = nk // SUBLANES
    s1t = s1.reshape(nt, SUBLANES, n)
    s2t = s2.reshape(nt, SUBLANES, n)
    a = _top16_sorted([s1t[k] for k in range(nt)])
    b = _top16_sorted([s2t[k] for k in range(nt)])
    sub = lax.broadcasted_iota(jnp.int32, (SUBLANES, n), 0)

    def pack(vals):
        out = vals[0]
        for r in range(1, SUBLANES):
            out = jnp.where(sub == r, vals[r], out)
        return out

    a_lo, a_hi, b_hi = pack(a[:8]), pack(a[8:]), pack(b[8:])
    cands = [a_lo + b[0], a_hi + b[0]]
    for y in range(1, 8):
        cands.append(jnp.where(sub <= K // (y + 1) - 1, a_lo + b[y], -jnp.inf))
    cands.append(a[0] + b_hi)
    cs = _top16_sorted(cands)
    tau = cs[K - 1]
    z = jnp.zeros_like(tau)
    for c in cs:
        z = z + jnp.exp(c - cs[0])
    n3 = jnp.zeros_like(tau)
    for c in cands:
        n3 = n3 + jnp.where(c >= tau, 1.0, 0.0)
    n3 = jnp.sum(n3, axis=0, keepdims=True)

    in1 = s1t >= a[K - 1]
    in2 = s2t >= b[K - 1]
    n1 = jnp.sum(jnp.sum(jnp.where(in1, 1.0, 0.0), axis=0), axis=0, keepdims=True)
    n2 = jnp.sum(jnp.sum(jnp.where(in2, 1.0, 0.0), axis=0), axis=0, keepdims=True)
    bad = jnp.where((n1 != float(K)) | (n2 != float(K)) | (n3 != float(K)), 1.0, 0.0)

    cnt_i = jnp.zeros_like(s1t)
    r2 = jnp.zeros_like(s2t)
    for y in range(K):
        cnt_i = cnt_i + jnp.where(s1t + b[y] >= tau, 1.0, 0.0)
        r2 = r2 + jnp.where(b[y] > s2t, 1.0, 0.0)
    cnt_i = jnp.where(in1, cnt_i, 0.0)
    ai = jnp.exp(s1t - a[0]) * (1.0 / z)
    bj = jnp.exp(s2t - b[0])
    return (r2.reshape(nk, n), cnt_i.reshape(nk, n), ai.reshape(nk, n), bj.reshape(nk, n), bad)


def _peer_select_kernel(h_ref, g_ref, wq_ref, keys_ref, xn_ref, r2_ref, bj_ref, cnt_ref, ai_ref,
                        q_ref):
    hidx = pl.program_id(1)

    @pl.when(hidx == 0)
    def _():
        xn = _rms(h_ref[...], g_ref[...]).astype(BF16)
        xn_ref[...] = xn
        q = _dot(xn, wq_ref[...]).astype(BF16)
        for g in range(2 * PEER_HEADS):
            q_ref[g] = q[:, g * PEER_HALF:(g + 1) * PEER_HALF]

    s1 = _dot_nt(keys_ref[0, 0], q_ref[2 * hidx])
    s2 = _dot_nt(keys_ref[0, 1], q_ref[2 * hidx + 1])

    def emit(r2, cnt_i, ai, bj):
        r2_ref[0] = r2.astype(r2_ref.dtype)
        cnt_ref[0] = cnt_i
        ai_ref[0] = ai
        bj_ref[0] = bj.astype(bj_ref.dtype)

    r2, cnt_i, ai, bj, bad = _route_sorted(s1, s2)
    emit(r2, cnt_i, ai, bj)

    @pl.when(jnp.max(bad) > 0.0)
    def _():
        emit(*_route_exact(s1, s2))


def _peer_select(h1, g, wq, keys, tt=256):
    t, d = h1.shape
    nk = PEER_N_KEYS
    stat = jax.ShapeDtypeStruct((PEER_HEADS, nk, t), F32)
    stat16 = jax.ShapeDtypeStruct((PEER_HEADS, nk, t), BF16)
    stat_spec = pl.BlockSpec((1, nk, tt), lambda i, h: (h, 0, i))
    return pl.pallas_call(
        _peer_select_kernel,
        grid=(t // tt, PEER_HEADS),
        in_specs=[pl.BlockSpec((tt, d), lambda i, h: (i, 0)),
                  pl.BlockSpec((1, d), lambda i, h: (0, 0)),
                  pl.BlockSpec((d, wq.shape[1]), lambda i, h: (0, 0)),
                  pl.BlockSpec((1, 2, nk, PEER_HALF), lambda i, h: (h, 0, 0, 0))],
        out_specs=[pl.BlockSpec((tt, d), lambda i, h: (i, 0)),
                   stat_spec, stat_spec, stat_spec, stat_spec],
        out_shape=[jax.ShapeDtypeStruct((t, d), BF16), stat16, stat16, stat, stat],
        scratch_shapes=[pltpu.VMEM((2 * PEER_HEADS, tt, PEER_HALF), BF16)],
        compiler_params=_params(("parallel", "arbitrary")),
        name="peer_select",
    )(h1, g, wq, keys)


def _gelu(a):
    return 0.5 * a * (1.0 + lax.erf(a * (1.0 / math.sqrt(2.0))))


def _peer_dense_kernel(xn_ref, u_ref, v_ref, r2_ref, bj_ref, cnt_ref, ai_ref, o_ref, m_ref):
    @pl.when(pl.program_id(1) == 0)
    def _():
        o_ref[...] = jnp.zeros_like(o_ref)

    tt = xn_ref.shape[0]
    et = u_ref.shape[0]
    nk = PEER_N_KEYS
    rows = 2 * SUBLANES
    tl = 2 * LANES
    zero = jnp.zeros((rows, tl), BF16)
    for il in range(et // nk):
        for tb in range(tt // tl):
            ls = slice(tb * tl, (tb + 1) * tl)
            acc = [None] * (nk // rows)
            for h in range(PEER_HEADS):
                c16 = jnp.broadcast_to(cnt_ref[il, h:h + 1, ls], (rows, tl)).astype(BF16)
                a16 = jnp.broadcast_to(ai_ref[il, h:h + 1, ls], (rows, tl)).astype(BF16)
                for jg in range(nk // rows):
                    js = slice(jg * rows, (jg + 1) * rows)
                    term = jnp.where(r2_ref[h, js, ls] < c16, a16 * bj_ref[h, js, ls], zero)
                    acc[jg] = term if acc[jg] is None else acc[jg] + term
            blk = jnp.concatenate(acc, axis=0).astype(F32)
            m_ref[ls, il * nk:(il + 1) * nk] = blk.T.astype(BF16)
    a = _dot_nt(xn_ref[...], u_ref[...])
    w = _gelu(a).astype(BF16) * m_ref[...]
    o_ref[...] += _dot(w, v_ref[...])


def _peer_dense(xn, u, v, r2, bj, cnt, ai, tt=1024, et=512):
    t, d = xn.shape
    ne = u.shape[0]
    nk = PEER_N_KEYS
    per_j = pl.BlockSpec((PEER_HEADS, nk, tt), lambda i, e: (0, 0, i))
    per_i = pl.BlockSpec((et // nk, PEER_HEADS, tt), lambda i, e: (e, 0, i))
    return pl.pallas_call(
        _peer_dense_kernel,
        grid=(t // tt, ne // et),
        in_specs=[pl.BlockSpec((tt, d), lambda i, e: (i, 0)),
                  pl.BlockSpec((et, d), lambda i, e: (e, 0)),
                  pl.BlockSpec((et, d), lambda i, e: (e, 0)),
                  per_j, per_j, per_i, per_i],
        out_specs=pl.BlockSpec((tt, d), lambda i, e: (i, 0)),
        out_shape=jax.ShapeDtypeStruct((t, d), F32),
        scratch_shapes=[pltpu.VMEM((tt, et), BF16)],
        compiler_params=_params(("parallel", "arbitrary")),
        name="peer_dense",
    )(xn, u, v, r2, bj, cnt, ai)


def _final_kernel(h1_ref, peer_ref, p_ref, g3_ref, wg_ref, wp_ref, gf_ref, o_ref, *, last):
    h2 = h1_ref[...] + peer_ref[...]
    xn = _rms(h2, g3_ref[...]).astype(BF16)
    gate = jax.nn.sigmoid(_dot(xn, wg_ref[...]))
    h3 = h2 + gate * _dot(p_ref[...].astype(BF16), wp_ref[...])
    o_ref[...] = _rms(h3, gf_ref[...]) if last else h3


def _final(h1, peer, p2, g3, wg, wp, gf, last, tm=256):
    t, d = h1.shape
    pd = p2.shape[1]
    row = pl.BlockSpec((tm, d), lambda i: (i, 0))
    vec = pl.BlockSpec((1, d), lambda i: (0, 0))
    return pl.pallas_call(
        functools.partial(_final_kernel, last=last),
        grid=(t // tm,),
        in_specs=[row, row, pl.BlockSpec((tm, pd), lambda i: (i, 0)), vec,
                  pl.BlockSpec((d, d), lambda i: (0, 0)),
                  pl.BlockSpec((pd, d), lambda i: (0, 0)), vec],
        out_specs=row,
        out_shape=jax.ShapeDtypeStruct((t, d), F32),
        compiler_params=_params(("parallel",)),
        name="ple_final",
    )(h1, peer, p2, g3, wg, wp, gf)


def kernel(x, p, norm_mix_g, w_in, conv_w, mlstm_gate_b, mlstm_norm_g, w_out, norm_ffn_g,
           peer_wq, peer_keys, peer_u, peer_v, norm_ple_g, ple_w_gate, ple_w_proj, norm_final_g):
    bsz, seq, d = x.shape
    depth = w_in.shape[0]
    t = bsz * seq
    h = x.reshape(t, d)
    for i in range(depth):
        w_main = w_in[i, :, :MAIN_COLS].astype(BF16)
        w_gates = jnp.pad(w_in[i, :, MAIN_COLS:], ((0, 0), (0, LANES - N_GATES))).astype(BF16)
        proj, gates = _rms_proj(h, norm_mix_g[i].reshape(1, d), w_main, w_gates)
        gate_b = mlstm_gate_b[i].astype(F32)
        mix = _mix(proj, gates, gates[:, :N_GATES].T, conv_w[i],
                   jnp.pad(gate_b, (0, LANES - N_GATES)).reshape(1, LANES),
                   gate_b.reshape(N_GATES, 1), mlstm_norm_g[i].reshape(1, MLSTM_DIM), bsz, seq)
        h1 = _matmul_res(mix, w_out[i].astype(BF16), h)
        xn, r2, bj, cnt, ai = _peer_select(h1, norm_ffn_g[i].reshape(1, d),
                                           peer_wq[i].astype(BF16), peer_keys[i].astype(BF16))
        peer = _peer_dense(xn, peer_u[i].astype(BF16), peer_v[i].astype(BF16), r2, bj,
                           jnp.transpose(cnt, (1, 0, 2)), jnp.transpose(ai, (1, 0, 2)))
        h = _final(h1, peer, p[i].reshape(t, -1), norm_ple_g[i].reshape(1, d),
                   ple_w_gate[i].astype(BF16), ple_w_proj[i].astype(BF16),
                   norm_final_g.reshape(1, d), i == depth - 1)
    return h.reshape(bsz, seq, d)
```

```python
import functools
import math

import jax
import jax.numpy as jnp
from jax import lax
from jax.experimental import pallas as pl
from jax.experimental.pallas import tpu as pltpu

F32 = jnp.float32
BF16 = jnp.bfloat16

D_MODEL = 2048
CONV_DIM = 1024
MLSTM_HEADS = 4
MLSTM_HEAD_DIM = 256
MLSTM_DIM = MLSTM_HEADS * MLSTM_HEAD_DIM
MLSTM_CHUNK = 128
MAIN_COLS = 3 * CONV_DIM + 4 * MLSTM_DIM
N_GATES = 2 * MLSTM_HEADS
PEER_HEADS = 8
PEER_N_KEYS = 128
PEER_HALF = 128
PEER_TOPK = 16
EPS = 1e-6

LANES = 128
VMEM_LIMIT = 60 * 1024 * 1024


def _params(semantics):
    return pltpu.CompilerParams(dimension_semantics=semantics, vmem_limit_bytes=VMEM_LIMIT)


def _rms(x, g):
    r = lax.rsqrt(jnp.mean(x * x, axis=-1, keepdims=True) + EPS)
    return (x * r) * g


def _dot(a, b):
    return jnp.dot(a, b, preferred_element_type=F32)


def _dot_nt(a, b):
    return lax.dot_general(a, b, (((1,), (1,)), ((), ())), preferred_element_type=F32)


def _rms_proj_kernel(x_ref, g_ref, w_ref, wg_ref, o_ref, og_ref, xn_ref):
    @pl.when(pl.program_id(1) == 0)
    def _():
        xn = _rms(x_ref[...], g_ref[...]).astype(BF16)
        xn_ref[...] = xn
        og_ref[...] = _dot(xn, wg_ref[...])

    o_ref[...] = _dot(xn_ref[...], w_ref[...])


def _rms_proj(x2, g, w_main, w_gate, tm=1024, tn=1024):
    t, d = x2.shape
    n = w_main.shape[1]
    return pl.pallas_call(
        _rms_proj_kernel,
        grid=(t // tm, n // tn),
        in_specs=[
            pl.BlockSpec((tm, d), lambda i, j: (i, 0)),
            pl.BlockSpec((1, d), lambda i, j: (0, 0)),
            pl.BlockSpec((d, tn), lambda i, j: (0, j)),
            pl.BlockSpec((d, LANES), lambda i, j: (0, 0)),
        ],
        out_specs=[
            pl.BlockSpec((tm, tn), lambda i, j: (i, j)),
            pl.BlockSpec((tm, LANES), lambda i, j: (i, 0)),
        ],
        out_shape=[
            jax.ShapeDtypeStruct((t, n), F32),
            jax.ShapeDtypeStruct((t, LANES), F32),
        ],
        scratch_shapes=[pltpu.VMEM((tm, d), BF16)],
        compiler_params=_params(("parallel", "arbitrary")),
        name="rms_proj",
    )(x2, g, w_main, w_gate)


def _log_sigmoid(x):
    return jnp.minimum(x, 0.0) - jnp.log1p(jnp.exp(-jnp.abs(x)))


def _mix_kernel(bc_ref, cc_ref, uc_ref, q_ref, k_ref, v_ref, o_ref, gc_ref, gr_ref,
                convw_ref, gbc_ref, gbr_ref, ng_ref, y_ref, caug_ref, m_ref, zc_ref):
    L = MLSTM_CHUNK
    hd = MLSTM_HEAD_DIM

    @pl.when(pl.program_id(1) == 0)
    def _():
        caug_ref[...] = jnp.zeros_like(caug_ref)
        m_ref[...] = jnp.zeros_like(m_ref)
        zc_ref[...] = jnp.zeros_like(zc_ref)

    z = cc_ref[...] * uc_ref[...]
    row = lax.broadcasted_iota(jnp.int32, z.shape, 0)
    prev = zc_ref[...]
    p1 = prev[7:8, :]
    p2 = prev[6:7, :]
    z1 = jnp.where(row == 0, p1, pltpu.roll(z, 1, 0))
    z2 = jnp.where(row == 0, p2, jnp.where(row == 1, p1, pltpu.roll(z, 2, 0)))
    cw = convw_ref[...]
    y_conv = bc_ref[...] * (cw[0:1, :] * z2 + cw[1:2, :] * z1 + cw[2:3, :] * z)
    zc_ref[...] = z[L - 8:, :]
    y_ref[:, :CONV_DIM] = y_conv.astype(y_ref.dtype)

    gcol = gc_ref[...] + gbc_ref[...]
    grow = gr_ref[...] + gbr_ref[...]
    ri = lax.broadcasted_iota(jnp.int32, (L, L), 0)
    ci = lax.broadcasted_iota(jnp.int32, (L, L), 1)
    causal = ri >= ci
    tril = causal.astype(F32)
    triu = (ri <= ci).astype(F32)
    hi = lax.Precision.HIGHEST
    b_col_all = jnp.dot(tril, _log_sigmoid(gcol), precision=hi, preferred_element_type=F32)
    b_row_all = jnp.dot(_log_sigmoid(grow), triu, precision=hi, preferred_element_type=F32)
    lane = lax.broadcasted_iota(jnp.int32, (L, LANES), 1)
    ones_col = (lane == 0).astype(BF16)
    scale = hd ** -0.5

    for h in range(MLSTM_HEADS):
        cs = slice(h * hd, (h + 1) * hd)
        qh = q_ref[:, cs].astype(BF16)
        kh = k_ref[:, cs]
        vaug = jnp.concatenate([v_ref[:, cs].astype(BF16), ones_col], axis=1)
        i_row = grow[h:h + 1, :]
        i_col = gcol[:, h:h + 1]
        b_row = b_row_all[MLSTM_HEADS + h:MLSTM_HEADS + h + 1, :]
        b_col = b_col_all[:, MLSTM_HEADS + h:MLSTM_HEADS + h + 1]
        m_prev = m_ref[h:h + 1, 0:1]

        dmat = jnp.where(causal, b_col - b_row + i_row, -jnp.inf)
        inter = b_col + m_prev
        m_t = jnp.maximum(inter, jnp.max(dmat, axis=-1, keepdims=True))
        pw = jnp.exp(dmat - m_t)
        sc = (_dot_nt(qh, kh.astype(BF16)) * scale) * pw
        a = jnp.exp(inter - m_t)
        caug = caug_ref[h]
        nd = a * _dot(qh, caug.astype(BF16)) + _dot(sc.astype(BF16), vaug)
        num = nd[:, :hd]
        den = nd[:, hd:hd + 1]
        hh = num / jnp.maximum(jnp.abs(den), jnp.exp(-m_t))
        hn = _rms(hh, ng_ref[:, cs])
        y_ref[:, CONV_DIM + h * hd:CONV_DIM + (h + 1) * hd] = (
            jax.nn.sigmoid(o_ref[:, cs]) * hn).astype(y_ref.dtype)

        m_new = m_t[L - 1:L, :]
        b_last = b_col[L - 1:L, :]
        w_s = jnp.exp(b_last - b_col + i_col - m_new)
        a_l = jnp.exp(b_last + m_prev - m_new)
        kw = (kh * (w_s * scale)).astype(BF16)
        caug_ref[h] = a_l * caug + _dot(kw.T, vaug)
        m_ref[h:h + 1, :] = jnp.broadcast_to(m_new, (1, LANES))


def _mix(proj, gates_col, gates_row, conv_w, gate_b_col, gate_b_row, norm_g, batch, seq):
    L = MLSTM_CHUNK
    nc = seq // L
    t = batch * seq
    blk = CONV_DIM

    def slab(k):
        return pl.BlockSpec((L, blk), lambda b, c, k=k: (b * nc + c, k))

    return pl.pallas_call(
        _mix_kernel,
        grid=(batch, nc),
        in_specs=[slab(0), slab(1), slab(2), slab(3), slab(4), slab(5), slab(6),
                  pl.BlockSpec((L, LANES), lambda b, c: (b * nc + c, 0)),
                  pl.BlockSpec((N_GATES, L), lambda b, c: (0, b * nc + c)),
                  pl.BlockSpec((3, CONV_DIM), lambda b, c: (0, 0)),
                  pl.BlockSpec((1, LANES), lambda b, c: (0, 0)),
                  pl.BlockSpec((N_GATES, 1), lambda b, c: (0, 0)),
                  pl.BlockSpec((1, MLSTM_DIM), lambda b, c: (0, 0))],
        out_specs=pl.BlockSpec((L, CONV_DIM + MLSTM_DIM), lambda b, c: (b * nc + c, 0)),
        out_shape=jax.ShapeDtypeStruct((t, CONV_DIM + MLSTM_DIM), BF16),
        scratch_shapes=[pltpu.VMEM((MLSTM_HEADS, MLSTM_HEAD_DIM, MLSTM_HEAD_DIM + LANES), F32),
                        pltpu.VMEM((8, LANES), F32),
                        pltpu.VMEM((8, CONV_DIM), F32)],
        compiler_params=_params(("parallel", "arbitrary")),
        name="mix",
    )(proj, proj, proj, proj, proj, proj, proj, gates_col, gates_row, conv_w,
      gate_b_col, gate_b_row, norm_g)


def _matmul_res_kernel(a_ref, w_ref, r_ref, o_ref):
    o_ref[...] = r_ref[...] + _dot(a_ref[...], w_ref[...])


def _matmul_res(a, w, res, tm=512):
    t, k = a.shape
    n = w.shape[1]
    return pl.pallas_call(
        _matmul_res_kernel,
        grid=(t // tm,),
        in_specs=[pl.BlockSpec((tm, k), lambda i: (i, 0)),
                  pl.BlockSpec((k, n), lambda i: (0, 0)),
                  pl.BlockSpec((tm, n), lambda i: (i, 0))],
        out_specs=pl.BlockSpec((tm, n), lambda i: (i, 0)),
        out_shape=jax.ShapeDtypeStruct((t, n), F32),
        compiler_params=_params(("parallel",)),
        name="matmul_res",
    )(a, w, res)


def _topk_rank(s, k):
    n = s.shape[0]
    iota = lax.broadcasted_iota(jnp.int32, s.shape, 0).astype(F32)
    rank = jnp.full(s.shape, float(k), F32)
    vals = []
    for r in range(k):
        mx = jnp.max(s, axis=0, keepdims=True)
        idx = jnp.min(jnp.where(s == mx, iota, float(n)), axis=0, keepdims=True)
        sel = iota == idx
        rank = jnp.where(sel, float(r), rank)
        s = jnp.where(sel, -jnp.inf, s)
        vals.append(mx)
    return rank, vals


def _route_exact(s1, s2):
    K = PEER_TOPK
    rank1, a_vals = _topk_rank(s1, K)
    rank2, b_vals = _topk_rank(s2, K)
    bmat = jnp.concatenate(b_vals, axis=0)
    cmat = jnp.concatenate([a_vals[x] + bmat for x in range(K)], axis=0)
    rank_c, c_vals = _topk_rank(cmat, K)
    chosen = (rank_c < float(K)).astype(F32)
    z = jnp.zeros_like(c_vals[0])
    for v in c_vals:
        z = z + jnp.exp(v - c_vals[0])
    cnt_i = jnp.zeros_like(s1)
    for x in range(K):
        cnt_x = jnp.sum(chosen[x * K:(x + 1) * K, :], axis=0, keepdims=True)
        cnt_i = cnt_i + jnp.where(rank1 == float(x), cnt_x, 0.0)
    return rank2, cnt_i, jnp.exp(s1 - a_vals[0]) / z, jnp.exp(s2 - b_vals[0])


def _sort16_network():
    def merge(lo, hi, r):
        step = r * 2
        if step < hi - lo:
            yield from merge(lo, hi, step)
            yield from merge(lo + r, hi, step)
            yield from [(i, i + r) for i in range(lo + r, hi - r, step)]
        else:
            yield (lo, lo + r)

    def sort(lo, hi):
        if hi - lo >= 1:
            mid = lo + (hi - lo) // 2
            yield from sort(lo, mid)
            yield from sort(mid + 1, hi)
            yield from merge(lo, hi, 1)

    return tuple(sort(0, 15))


_SORT16 = _sort16_network()
SUBLANES = 8


def _top16_sorted(tiles):
    v = list(tiles) + [None] * (16 - len(tiles))
    for i, j in _SORT16:
        a, b = v[i], v[j]
        if b is None:
            continue
        if a is None:
            v[i], v[j] = b, None
        else:
            v[i], v[j] = jnp.maximum(a, b), jnp.minimum(a, b)
    neg = jnp.full(tiles[0].shape, -jnp.inf, F32)
    v = [neg if x is None else x for x in v]
    for shift in (4, 2, 1):
        r = [pltpu.roll(x, shift, 0) for x in v]
        v = [jnp.maximum(v[k], r[15 - k]) for k in range(16)]
        d = 8
        while d >= 1:
            for k in range(16):
                if (k & d) == 0:
                    v[k], v[k + d] = jnp.maximum(v[k], v[k + d]), jnp.minimum(v[k], v[k + d])
            d //= 2
    return v


def _route_sorted(s1, s2):
    K = PEER_TOPK
    nk, n = s1.shape
    nt = nk // SUBLANES
    s1t = s1.reshape(nt, SUBLANES, n)
    s2t = s2.reshape(nt, SUBLANES, n)
    a = _top16_sorted([s1t[k] for k in range(nt)])
    b = _top16_sorted([s2t[k] for k in range(nt)])
    sub = lax.broadcasted_iota(jnp.int32, (SUBLANES, n), 0)

    def pack(vals):
        out = vals[0]
        for r in range(1, SUBLANES):
            out = jnp.where(sub == r, vals[r], out)
        return out

    a_lo, a_hi, b_hi = pack(a[:8]), pack(a[8:]), pack(b[8:])
    cands = [a_lo + b[0], a_hi + b[0]]
    for y in range(1, 8):
        cands.append(jnp.where(sub <= K // (y + 1) - 1, a_lo + b[y], -jnp.inf))
    cands.append(a[0] + b_hi)
    cs = _top16_sorted(cands)
    tau = cs[K - 1]
    z = jnp.zeros_like(tau)
    for c in cs:
        z = z + jnp.exp(c - cs[0])
    n3 = jnp.zeros_like(tau)
    for c in cands:
        n3 = n3 + jnp.where(c >= tau, 1.0, 0.0)
    n3 = jnp.sum(n3, axis=0, keepdims=True)

    in1 = s1t >= a[K - 1]
    in2 = s2t >= b[K - 1]
    n1 = jnp.sum(jnp.sum(jnp.where(in1, 1.0, 0.0), axis=0), axis=0, keepdims=True)
    n2 = jnp.sum(jnp.sum(jnp.where(in2, 1.0, 0.0), axis=0), axis=0, keepdims=True)
    bad = jnp.where((n1 != float(K)) | (n2 != float(K)) | (n3 != float(K)), 1.0, 0.0)

    cnt_i = jnp.zeros_like(s1t)
    r2 = jnp.zeros_like(s2t)
    for y in range(K):
        cnt_i = cnt_i + jnp.where(s1t + b[y] >= tau, 1.0, 0.0)
        r2 = r2 + jnp.where(b[y] > s2t, 1.0, 0.0)
    cnt_i = jnp.where(in1, cnt_i, 0.0)
    ai = jnp.exp(s1t - a[0]) * (1.0 / z)
    bj = jnp.exp(s2t - b[0])
    return (r2.reshape(nk, n), cnt_i.reshape(nk, n), ai.reshape(nk, n), bj.reshape(nk, n), bad)


def _peer_select_kernel(h_ref, g_ref, wq_ref, keys_ref, xn_ref, r2_ref, bj_ref, cnt_ref, ai_ref,
                        q_ref):
    hidx = pl.program_id(1)

    @pl.when(hidx == 0)
    def _():
        xn = _rms(h_ref[...], g_ref[...]).astype(BF16)
        xn_ref[...] = xn
        q = _dot(xn, wq_ref[...]).astype(BF16)
        for g in range(2 * PEER_HEADS):
            q_ref[g] = q[:, g * PEER_HALF:(g + 1) * PEER_HALF]

    s1 = _dot_nt(keys_ref[0, 0], q_ref[2 * hidx])
    s2 = _dot_nt(keys_ref[0, 1], q_ref[2 * hidx + 1])

    def pair_bits(x):
        b = lax.bitcast_convert_type(x, jnp.int32)
        return b | lax.shift_right_logical(b, 16)

    def emit(r2, cnt_i, ai, bj):
        for k in range(r2.shape[1] // LANES):
            ls = slice(k * LANES, (k + 1) * LANES)
            r2_ref[0, k] = r2[:, ls].astype(BF16)
            bj_ref[0, k] = bj[:, ls].astype(BF16)
        cnt_ref[0] = pair_bits(cnt_i)
        ai_ref[0] = ai

    r2, cnt_i, ai, bj, bad = _route_sorted(s1, s2)
    emit(r2, cnt_i, ai, bj)

    @pl.when(jnp.max(bad) > 0.0)
    def _():
        emit(*_route_exact(s1, s2))


def _peer_select(h1, g, wq, keys, tt=256):
    t, d = h1.shape
    nk = PEER_N_KEYS
    stat_spec = pl.BlockSpec((1, nk, tt), lambda i, h: (h, 0, i))
    stat16 = jax.ShapeDtypeStruct((PEER_HEADS, t // LANES, nk, LANES), BF16)
    stat16_spec = pl.BlockSpec((1, tt // LANES, nk, LANES), lambda i, h: (h, i, 0, 0))
    return pl.pallas_call(
        _peer_select_kernel,
        grid=(t // tt, PEER_HEADS),
        in_specs=[pl.BlockSpec((tt, d), lambda i, h: (i, 0)),
                  pl.BlockSpec((1, d), lambda i, h: (0, 0)),
                  pl.BlockSpec((d, wq.shape[1]), lambda i, h: (0, 0)),
                  pl.BlockSpec((1, 2, nk, PEER_HALF), lambda i, h: (h, 0, 0, 0))],
        out_specs=[pl.BlockSpec((tt, d), lambda i, h: (i, 0)),
                   stat16_spec, stat16_spec, stat_spec, stat_spec],
        out_shape=[jax.ShapeDtypeStruct((t, d), BF16), stat16, stat16,
                   jax.ShapeDtypeStruct((PEER_HEADS, nk, t), jnp.int32),
                   jax.ShapeDtypeStruct((PEER_HEADS, nk, t), F32)],
        scratch_shapes=[pltpu.VMEM((2 * PEER_HEADS, tt, PEER_HALF), BF16)],
        compiler_params=_params(("parallel", "arbitrary")),
        name="peer_select",
    )(h1, g, wq, keys)


def _gelu(a):
    return 0.5 * a * (1.0 + lax.erf(a * (1.0 / math.sqrt(2.0))))


def _peer_dense_kernel(xn_ref, u_ref, v_ref, r2_in_ref, bj_in_ref, cnt_ref, ai_ref, o_ref,
                       m_ref, r2_ref, bj_ref):
    @pl.when(pl.program_id(1) == 0)
    def _():
        o_ref[...] = jnp.zeros_like(o_ref)
        r2_ref[...] = r2_in_ref[...]
        bj_ref[...] = bj_in_ref[...]

    tt = xn_ref.shape[0]
    et = u_ref.shape[0]
    nk = PEER_N_KEYS
    rows = 2 * SUBLANES
    tl = LANES
    zero = jnp.zeros((rows, tl), BF16)
    for il in range(et // nk):
        for tb in range(tt // tl):
            ls = slice(tb * tl, (tb + 1) * tl)
            acc = [None] * (nk // rows)
            for h in range(PEER_HEADS):
                c16 = pltpu.bitcast(jnp.broadcast_to(cnt_ref[il, h:h + 1, ls], (SUBLANES, tl)), BF16)
                a16 = jnp.broadcast_to(ai_ref[il, h:h + 1, ls], (rows, tl)).astype(BF16)
                for jg in range(nk // rows):
                    js = slice(jg * rows, (jg + 1) * rows)
                    term = jnp.where(r2_ref[h, tb, js, :] < c16, a16 * bj_ref[h, tb, js, :], zero)
                    acc[jg] = term if acc[jg] is None else acc[jg] + term
            m_ref[ls, il * nk:(il + 1) * nk] = jnp.concatenate(acc, axis=0).T
    a = _dot_nt(xn_ref[...], u_ref[...].astype(BF16))
    w = _gelu(a).astype(BF16) * m_ref[...]
    o_ref[...] += _dot(w, v_ref[...].astype(BF16))


def _peer_dense(xn, u, v, r2, bj, cnt, ai, tt=1024, et=512):
    t, d = xn.shape
    nk = PEER_N_KEYS
    per_j = pl.BlockSpec((PEER_HEADS, tt // LANES, nk, LANES), lambda i, e: (0, i, 0, 0))
    per_i = pl.BlockSpec((et // nk, PEER_HEADS, tt), lambda i, e: (e, 0, i))
    return pl.pallas_call(
        _peer_dense_kernel,
        grid=(t // tt, u.shape[0] // et),
        in_specs=[pl.BlockSpec((tt, d), lambda i, e: (i, 0)),
                  pl.BlockSpec((et, d), lambda i, e: (e, 0)),
                  pl.BlockSpec((et, d), lambda i, e: (e, 0)),
                  per_j, per_j, per_i, per_i],
        out_specs=pl.BlockSpec((tt, d), lambda i, e: (i, 0)),
        out_shape=jax.ShapeDtypeStruct((t, d), F32),
        scratch_shapes=[pltpu.VMEM((tt, et), BF16),
                        pltpu.VMEM((PEER_HEADS, tt // LANES, nk, LANES), BF16),
                        pltpu.VMEM((PEER_HEADS, tt // LANES, nk, LANES), BF16)],
        compiler_params=_params(("parallel", "arbitrary")),
        name="peer_dense",
    )(xn, u, v, r2, bj, cnt, ai)


def _final_kernel(h1_ref, peer_ref, p_ref, g3_ref, wg_ref, wp_ref, gf_ref, o_ref, *, last):
    h2 = h1_ref[...] + peer_ref[...]
    xn = _rms(h2, g3_ref[...]).astype(BF16)
    gate = jax.nn.sigmoid(_dot(xn, wg_ref[...]))
    h3 = h2 + gate * _dot(p_ref[...].astype(BF16), wp_ref[...])
    o_ref[...] = _rms(h3, gf_ref[...]) if last else h3


def _final(h1, peer, p2, g3, wg, wp, gf, last, tm=256):
    t, d = h1.shape
    pd = p2.shape[1]
    row = pl.BlockSpec((tm, d), lambda i: (i, 0))
    vec = pl.BlockSpec((1, d), lambda i: (0, 0))
    return pl.pallas_call(
        functools.partial(_final_kernel, last=last),
        grid=(t // tm,),
        in_specs=[row, row, pl.BlockSpec((tm, pd), lambda i: (i, 0)), vec,
                  pl.BlockSpec((d, d), lambda i: (0, 0)),
                  pl.BlockSpec((pd, d), lambda i: (0, 0)), vec],
        out_specs=row,
        out_shape=jax.ShapeDtypeStruct((t, d), F32),
        compiler_params=_params(("parallel",)),
        name="ple_final",
    )(h1, peer, p2, g3, wg, wp, gf)


def kernel(x, p, norm_mix_g, w_in, conv_w, mlstm_gate_b, mlstm_norm_g, w_out, norm_ffn_g,
           peer_wq, peer_keys, peer_u, peer_v, norm_ple_g, ple_w_gate, ple_w_proj, norm_final_g):
    bsz, seq, d = x.shape
    depth = w_in.shape[0]
    t = bsz * seq
    h = x.reshape(t, d)
    for i in range(depth):
        w_main = w_in[i, :, :MAIN_COLS].astype(BF16)
        w_gates = jnp.pad(w_in[i, :, MAIN_COLS:], ((0, 0), (0, LANES - N_GATES))).astype(BF16)
        proj, gates = _rms_proj(h, norm_mix_g[i].reshape(1, d), w_main, w_gates)
        gate_b = mlstm_gate_b[i].astype(F32)
        mix = _mix(proj, gates, gates[:, :N_GATES].T, conv_w[i],
                   jnp.pad(gate_b, (0, LANES - N_GATES)).reshape(1, LANES),
                   gate_b.reshape(N_GATES, 1), mlstm_norm_g[i].reshape(1, MLSTM_DIM), bsz, seq)
        h1 = _matmul_res(mix, w_out[i].astype(BF16), h)
        xn, r2, bj, cnt, ai = _peer_select(h1, norm_ffn_g[i].reshape(1, d),
                                           peer_wq[i].astype(BF16), peer_keys[i].astype(BF16))
        peer = _peer_dense(xn, peer_u[i], peer_v[i], r2, bj,
                           jnp.transpose(cnt, (1, 0, 2)), jnp.transpose(ai, (1, 0, 2)))
        h = _final(h1, peer, p[i].reshape(t, -1), norm_ple_g[i].reshape(1, d),
                   ple_w_gate[i].astype(BF16), ple_w_proj[i].astype(BF16),
                   norm_final_g.reshape(1, d), i == depth - 1)
    return h.reshape(bsz, seq, d)
```

```python
import functools
import math

import jax
import jax.numpy as jnp
from jax import lax
from jax.experimental import pallas as pl
from jax.experimental.pallas import tpu as pltpu

F32 = jnp.float32
BF16 = jnp.bfloat16

D_MODEL = 2048
CONV_DIM = 1024
MLSTM_HEADS = 4
MLSTM_HEAD_DIM = 256
MLSTM_DIM = MLSTM_HEADS * MLSTM_HEAD_DIM
MLSTM_CHUNK = 128
MAIN_COLS = 3 * CONV_DIM + 4 * MLSTM_DIM
N_GATES = 2 * MLSTM_HEADS
PEER_HEADS = 8
PEER_N_KEYS = 128
PEER_HALF = 128
PEER_TOPK = 16
EPS = 1e-6

LANES = 128
VMEM_LIMIT = 60 * 1024 * 1024


def _params(semantics):
    return pltpu.CompilerParams(dimension_semantics=semantics, vmem_limit_bytes=VMEM_LIMIT)


def _rms(x, g):
    r = lax.rsqrt(jnp.mean(x * x, axis=-1, keepdims=True) + EPS)
    return (x * r) * g


def _dot(a, b):
    return jnp.dot(a, b, preferred_element_type=F32)


def _dot_nt(a, b):
    return lax.dot_general(a, b, (((1,), (1,)), ((), ())), preferred_element_type=F32)


def _rms_proj_kernel(x_ref, g_ref, w_ref, wg_ref, o_ref, og_ref, xn_ref):
    @pl.when(pl.program_id(1) == 0)
    def _():
        xn = _rms(x_ref[...], g_ref[...]).astype(BF16)
        xn_ref[...] = xn
        og_ref[...] = _dot(xn, wg_ref[...])

    o_ref[...] = _dot(xn_ref[...], w_ref[...])


def _rms_proj(x2, g, w_main, w_gate, tm=1024, tn=1024):
    t, d = x2.shape
    n = w_main.shape[1]
    return pl.pallas_call(
        _rms_proj_kernel,
        grid=(t // tm, n // tn),
        in_specs=[
            pl.BlockSpec((tm, d), lambda i, j: (i, 0)),
            pl.BlockSpec((1, d), lambda i, j: (0, 0)),
            pl.BlockSpec((d, tn), lambda i, j: (0, j)),
            pl.BlockSpec((d, LANES), lambda i, j: (0, 0)),
        ],
        out_specs=[
            pl.BlockSpec((tm, tn), lambda i, j: (i, j)),
            pl.BlockSpec((tm, LANES), lambda i, j: (i, 0)),
        ],
        out_shape=[
            jax.ShapeDtypeStruct((t, n), F32),
            jax.ShapeDtypeStruct((t, LANES), F32),
        ],
        scratch_shapes=[pltpu.VMEM((tm, d), BF16)],
        compiler_params=_params(("parallel", "arbitrary")),
        name="rms_proj",
    )(x2, g, w_main, w_gate)


def _log_sigmoid(x):
    return jnp.minimum(x, 0.0) - jnp.log1p(jnp.exp(-jnp.abs(x)))


def _mix_kernel(bc_ref, cc_ref, uc_ref, q_ref, k_ref, v_ref, o_ref, gc_ref, gr_ref,
                convw_ref, gbc_ref, gbr_ref, ng_ref, y_ref, caug_ref, m_ref, zc_ref):
    L = MLSTM_CHUNK
    hd = MLSTM_HEAD_DIM

    @pl.when(pl.program_id(1) == 0)
    def _():
        caug_ref[...] = jnp.zeros_like(caug_ref)
        m_ref[...] = jnp.zeros_like(m_ref)
        zc_ref[...] = jnp.zeros_like(zc_ref)

    z = cc_ref[...] * uc_ref[...]
    row = lax.broadcasted_iota(jnp.int32, z.shape, 0)
    prev = zc_ref[...]
    p1 = prev[7:8, :]
    p2 = prev[6:7, :]
    z1 = jnp.where(row == 0, p1, pltpu.roll(z, 1, 0))
    z2 = jnp.where(row == 0, p2, jnp.where(row == 1, p1, pltpu.roll(z, 2, 0)))
    cw = convw_ref[...]
    y_conv = bc_ref[...] * (cw[0:1, :] * z2 + cw[1:2, :] * z1 + cw[2:3, :] * z)
    zc_ref[...] = z[L - 8:, :]
    y_ref[:, :CONV_DIM] = y_conv.astype(y_ref.dtype)

    gcol = gc_ref[...] + gbc_ref[...]
    grow = gr_ref[...] + gbr_ref[...]
    ri = lax.broadcasted_iota(jnp.int32, (L, L), 0)
    ci = lax.broadcasted_iota(jnp.int32, (L, L), 1)
    causal = ri >= ci
    tril = causal.astype(F32)
    triu = (ri <= ci).astype(F32)
    hi = lax.Precision.HIGHEST
    b_col_all = jnp.dot(tril, _log_sigmoid(gcol), precision=hi, preferred_element_type=F32)
    b_row_all = jnp.dot(_log_sigmoid(grow), triu, precision=hi, preferred_element_type=F32)
    lane = lax.broadcasted_iota(jnp.int32, (L, LANES), 1)
    ones_col = (lane == 0).astype(BF16)
    scale = hd ** -0.5

    for h in range(MLSTM_HEADS):
        cs = slice(h * hd, (h + 1) * hd)
        qh = q_ref[:, cs].astype(BF16)
        kh = k_ref[:, cs]
        vaug = jnp.concatenate([v_ref[:, cs].astype(BF16), ones_col], axis=1)
        i_row = grow[h:h + 1, :]
        i_col = gcol[:, h:h + 1]
        b_row = b_row_all[MLSTM_HEADS + h:MLSTM_HEADS + h + 1, :]
        b_col = b_col_all[:, MLSTM_HEADS + h:MLSTM_HEADS + h + 1]
        m_prev = m_ref[h:h + 1, 0:1]

        dmat = jnp.where(causal, b_col - b_row + i_row, -jnp.inf)
        inter = b_col + m_prev
        m_t = jnp.maximum(inter, jnp.max(dmat, axis=-1, keepdims=True))
        pw = jnp.exp(dmat - m_t)
        sc = (_dot_nt(qh, kh.astype(BF16)) * scale) * pw
        a = jnp.exp(inter - m_t)
        caug = caug_ref[h]
        nd = a * _dot(qh, caug.astype(BF16)) + _dot(sc.astype(BF16), vaug)
        num = nd[:, :hd]
        den = nd[:, hd:hd + 1]
        hh = num / jnp.maximum(jnp.abs(den), jnp.exp(-m_t))
        hn = _rms(hh, ng_ref[:, cs])
        y_ref[:, CONV_DIM + h * hd:CONV_DIM + (h + 1) * hd] = (
            jax.nn.sigmoid(o_ref[:, cs]) * hn).astype(y_ref.dtype)

        m_new = m_t[L - 1:L, :]
        b_last = b_col[L - 1:L, :]
        w_s = jnp.exp(b_last - b_col + i_col - m_new)
        a_l = jnp.exp(b_last + m_prev - m_new)
        kw = (kh * (w_s * scale)).astype(BF16)
        caug_ref[h] = a_l * caug + _dot(kw.T, vaug)
        m_ref[h:h + 1, :] = jnp.broadcast_to(m_new, (1, LANES))


def _mix(proj, gates_col, gates_row, conv_w, gate_b_col, gate_b_row, norm_g, batch, seq):
    L = MLSTM_CHUNK
    nc = seq // L
    t = batch * seq
    blk = CONV_DIM

    def slab(k):
        return pl.BlockSpec((L, blk), lambda b, c, k=k: (b * nc + c, k))

    return pl.pallas_call(
        _mix_kernel,
        grid=(batch, nc),
        in_specs=[slab(0), slab(1), slab(2), slab(3), slab(4), slab(5), slab(6),
                  pl.BlockSpec((L, LANES), lambda b, c: (b * nc + c, 0)),
                  pl.BlockSpec((N_GATES, L), lambda b, c: (0, b * nc + c)),
                  pl.BlockSpec((3, CONV_DIM), lambda b, c: (0, 0)),
                  pl.BlockSpec((1, LANES), lambda b, c: (0, 0)),
                  pl.BlockSpec((N_GATES, 1), lambda b, c: (0, 0)),
                  pl.BlockSpec((1, MLSTM_DIM), lambda b, c: (0, 0))],
        out_specs=pl.BlockSpec((L, CONV_DIM + MLSTM_DIM), lambda b, c: (b * nc + c, 0)),
        out_shape=jax.ShapeDtypeStruct((t, CONV_DIM + MLSTM_DIM), BF16),
        scratch_shapes=[pltpu.VMEM((MLSTM_HEADS, MLSTM_HEAD_DIM, MLSTM_HEAD_DIM + LANES), F32),
                        pltpu.VMEM((8, LANES), F32),
                        pltpu.VMEM((8, CONV_DIM), F32)],
        compiler_params=_params(("parallel", "arbitrary")),
        name="mix",
    )(proj, proj, proj, proj, proj, proj, proj, gates_col, gates_row, conv_w,
      gate_b_col, gate_b_row, norm_g)


def _matmul_res_kernel(a_ref, w_ref, r_ref, o_ref):
    o_ref[...] = r_ref[...] + _dot(a_ref[...], w_ref[...])


def _matmul_res(a, w, res, tm=512):
    t, k = a.shape
    n = w.shape[1]
    return pl.pallas_call(
        _matmul_res_kernel,
        grid=(t // tm,),
        in_specs=[pl.BlockSpec((tm, k), lambda i: (i, 0)),
                  pl.BlockSpec((k, n), lambda i: (0, 0)),
                  pl.BlockSpec((tm, n), lambda i: (i, 0))],
        out_specs=pl.BlockSpec((tm, n), lambda i: (i, 0)),
        out_shape=jax.ShapeDtypeStruct((t, n), F32),
        compiler_params=_params(("parallel",)),
        name="matmul_res",
    )(a, w, res)


def _topk_rank(s, k):
    n = s.shape[0]
    iota = lax.broadcasted_iota(jnp.int32, s.shape, 0).astype(F32)
    rank = jnp.full(s.shape, float(k), F32)
    vals = []
    for r in range(k):
        mx = jnp.max(s, axis=0, keepdims=True)
        idx = jnp.min(jnp.where(s == mx, iota, float(n)), axis=0, keepdims=True)
        sel = iota == idx
        rank = jnp.where(sel, float(r), rank)
        s = jnp.where(sel, -jnp.inf, s)
        vals.append(mx)
    return rank, vals


def _route_exact(s1, s2):
    K = PEER_TOPK
    rank1, a_vals = _topk_rank(s1, K)
    rank2, b_vals = _topk_rank(s2, K)
    bmat = jnp.concatenate(b_vals, axis=0)
    cmat = jnp.concatenate([a_vals[x] + bmat for x in range(K)], axis=0)
    rank_c, c_vals = _topk_rank(cmat, K)
    chosen = (rank_c < float(K)).astype(F32)
    z = jnp.zeros_like(c_vals[0])
    for v in c_vals:
        z = z + jnp.exp(v - c_vals[0])
    cnt_i = jnp.zeros_like(s1)
    for x in range(K):
        cnt_x = jnp.sum(chosen[x * K:(x + 1) * K, :], axis=0, keepdims=True)
        cnt_i = cnt_i + jnp.where(rank1 == float(x), cnt_x, 0.0)
    return rank2, cnt_i, jnp.exp(s1 - a_vals[0]) / z, jnp.exp(s2 - b_vals[0])


def _sort16_network():
    def merge(lo, hi, r):
        step = r * 2
        if step < hi - lo:
            yield from merge(lo, hi, step)
            yield from merge(lo + r, hi, step)
            yield from [(i, i + r) for i in range(lo + r, hi - r, step)]
        else:
            yield (lo, lo + r)

    def sort(lo, hi):
        if hi - lo >= 1:
            mid = lo + (hi - lo) // 2
            yield from sort(lo, mid)
            yield from sort(mid + 1, hi)
            yield from merge(lo, hi, 1)

    return tuple(sort(0, 15))


_SORT16 = _sort16_network()
SUBLANES = 8


def _top16_sorted(tiles):
    v = list(tiles) + [None] * (16 - len(tiles))
    for i, j in _SORT16:
        a, b = v[i], v[j]
        if b is None:
            continue
        if a is None:
            v[i], v[j] = b, None
        else:
            v[i], v[j] = jnp.maximum(a, b), jnp.minimum(a, b)
    neg = jnp.full(tiles[0].shape, -jnp.inf, F32)
    v = [neg if x is None else x for x in v]
    for shift in (4, 2, 1):
        r = [pltpu.roll(x, shift, 0) for x in v]
        v = [jnp.maximum(v[k], r[15 - k]) for k in range(16)]
        d = 8
        while d >= 1:
            for k in range(16):
                if (k & d) == 0:
                    v[k], v[k + d] = jnp.maximum(v[k], v[k + d]), jnp.minimum(v[k], v[k + d])
            d //= 2
    return v


def _route_sorted(s1, s2):
    K = PEER_TOPK
    nk, n = s1.shape
    nt = nk // SUBLANES
    s1t = s1.reshape(nt, SUBLANES, n)
    s2t = s2.reshape(nt, SUBLANES, n)
    a = _top16_sorted([s1t[k] for k in range(nt)])
    b = _top16_sorted([s2t[k] for k in range(nt)])
    sub = lax.broadcasted_iota(jnp.int32, (SUBLANES, n), 0)

    def pack(vals):
        out = vals[0]
        for r in range(1, SUBLANES):
            out = jnp.where(sub == r, vals[r], out)
        return out

    a_lo, a_hi, b_hi = pack(a[:8]), pack(a[8:]), pack(b[8:])
    cands = [a_lo + b[0], a_hi + b[0]]
    for y in range(1, 8):
        cands.append(jnp.where(sub <= K // (y + 1) - 1, a_lo + b[y], -jnp.inf))
    cands.append(a[0] + b_hi)
    cs = _top16_sorted(cands)
    tau = cs[K - 1]
    z = jnp.zeros_like(tau)
    for c in cs:
        z = z + jnp.exp(c - cs[0])
    n3 = jnp.zeros_like(tau)
    for c in cands:
        n3 = n3 + jnp.where(c >= tau, 1.0, 0.0)
    n3 = jnp.sum(n3, axis=0, keepdims=True)

    in1 = s1t >= a[K - 1]
    in2 = s2t >= b[K - 1]
    n1 = jnp.sum(jnp.sum(jnp.where(in1, 1.0, 0.0), axis=0), axis=0, keepdims=True)
    n2 = jnp.sum(jnp.sum(jnp.where(in2, 1.0, 0.0), axis=0), axis=0, keepdims=True)
    bad = jnp.where((n1 != float(K)) | (n2 != float(K)) | (n3 != float(K)), 1.0, 0.0)

    cnt_i = jnp.zeros_like(s1t)
    r2 = jnp.zeros_like(s2t)
    for y in range(K):
        cnt_i = cnt_i + jnp.where(s1t + b[y] >= tau, 1.0, 0.0)
        r2 = r2 + jnp.where(b[y] > s2t, 1.0, 0.0)
    cnt_i = jnp.where(in1, cnt_i, 0.0)
    ai = jnp.exp(s1t - a[0]) * (1.0 / z)
    bj = jnp.exp(s2t - b[0])
    return (r2.reshape(nk, n), cnt_i.reshape(nk, n), ai.reshape(nk, n), bj.reshape(nk, n), bad)


def _peer_select_kernel(h_ref, g_ref, wq_ref, keys_ref, xn_ref, r2_ref, bj_ref, cnt_ref, ai_ref,
                        q_ref):
    hidx = pl.program_id(1)

    @pl.when(hidx == 0)
    def _():
        xn = _rms(h_ref[...], g_ref[...]).astype(BF16)
        xn_ref[...] = xn
        q = _dot(xn, wq_ref[...]).astype(BF16)
        for g in range(2 * PEER_HEADS):
            q_ref[g] = q[:, g * PEER_HALF:(g + 1) * PEER_HALF]

    s1 = _dot_nt(keys_ref[0, 0], q_ref[2 * hidx])
    s2 = _dot_nt(keys_ref[0, 1], q_ref[2 * hidx + 1])

    def pair_bits(x):
        b = lax.bitcast_convert_type(x, jnp.int32)
        return b | lax.shift_right_logical(b, 16)

    def emit(r2, cnt_i, ai, bj):
        for k in range(r2.shape[1] // LANES):
            ls = slice(k * LANES, (k + 1) * LANES)
            r2_ref[0, k] = r2[:, ls].astype(BF16)
            bj_ref[0, k] = bj[:, ls].astype(BF16)
        cw = pair_bits(cnt_i)
        for k in range(cw.shape[0] // SUBLANES):
            rs = slice(k * SUBLANES, (k + 1) * SUBLANES)
            cnt_ref[pl.ds(k * SUBLANES, SUBLANES), hidx, :] = cw[rs]
            ai_ref[pl.ds(k * SUBLANES, SUBLANES), hidx, :] = ai[rs]

    r2, cnt_i, ai, bj, bad = _route_sorted(s1, s2)
    emit(r2, cnt_i, ai, bj)

    @pl.when(jnp.max(bad) > 0.0)
    def _():
        emit(*_route_exact(s1, s2))


def _peer_select(h1, g, wq, keys, tt=256):
    t, d = h1.shape
    nk = PEER_N_KEYS
    stat_spec = pl.BlockSpec((nk, PEER_HEADS, tt), lambda i, h: (0, 0, i))
    stat16 = jax.ShapeDtypeStruct((PEER_HEADS, t // LANES, nk, LANES), BF16)
    stat16_spec = pl.BlockSpec((1, tt // LANES, nk, LANES), lambda i, h: (h, i, 0, 0))
    return pl.pallas_call(
        _peer_select_kernel,
        grid=(t // tt, PEER_HEADS),
        in_specs=[pl.BlockSpec((tt, d), lambda i, h: (i, 0)),
                  pl.BlockSpec((1, d), lambda i, h: (0, 0)),
                  pl.BlockSpec((d, wq.shape[1]), lambda i, h: (0, 0)),
                  pl.BlockSpec((1, 2, nk, PEER_HALF), lambda i, h: (h, 0, 0, 0))],
        out_specs=[pl.BlockSpec((tt, d), lambda i, h: (i, 0)),
                   stat16_spec, stat16_spec, stat_spec, stat_spec],
        out_shape=[jax.ShapeDtypeStruct((t, d), BF16), stat16, stat16,
                   jax.ShapeDtypeStruct((nk, PEER_HEADS, t), jnp.int32),
                   jax.ShapeDtypeStruct((nk, PEER_HEADS, t), F32)],
        scratch_shapes=[pltpu.VMEM((2 * PEER_HEADS, tt, PEER_HALF), BF16)],
        compiler_params=_params(("parallel", "arbitrary")),
        name="peer_select",
    )(h1, g, wq, keys)


def _gelu(a):
    return 0.5 * a * (1.0 + lax.erf(a * (1.0 / math.sqrt(2.0))))


def _peer_dense_kernel(xn_ref, u_ref, v_ref, r2_in_ref, bj_in_ref, cnt_ref, ai_ref, o_ref,
                       m_ref, r2_ref, bj_ref):
    @pl.when(pl.program_id(1) == 0)
    def _():
        o_ref[...] = jnp.zeros_like(o_ref)
        r2_ref[...] = r2_in_ref[...]
        bj_ref[...] = bj_in_ref[...]

    tt = xn_ref.shape[0]
    et = u_ref.shape[0]
    nk = PEER_N_KEYS
    rows = 2 * SUBLANES
    tl = LANES
    zero = jnp.zeros((rows, tl), BF16)
    for il in range(et // nk):
        for tb in range(tt // tl):
            ls = slice(tb * tl, (tb + 1) * tl)
            acc = [None] * (nk // rows)
            for h in range(PEER_HEADS):
                c16 = pltpu.bitcast(jnp.broadcast_to(cnt_ref[il, h:h + 1, ls], (SUBLANES, tl)), BF16)
                a16 = jnp.broadcast_to(ai_ref[il, h:h + 1, ls], (rows, tl)).astype(BF16)
                for jg in range(nk // rows):
                    js = slice(jg * rows, (jg + 1) * rows)
                    term = jnp.where(r2_ref[h, tb, js, :] < c16, a16 * bj_ref[h, tb, js, :], zero)
                    acc[jg] = term if acc[jg] is None else acc[jg] + term
            m_ref[ls, il * nk:(il + 1) * nk] = jnp.concatenate(acc, axis=0).T
    a = _dot_nt(xn_ref[...], u_ref[...])
    w = _gelu(a).astype(BF16) * m_ref[...]
    o_ref[...] += _dot(w, v_ref[...])


def _peer_dense(xn, u, v, r2, bj, cnt, ai, tt=1024, et=512):
    t, d = xn.shape
    nk = PEER_N_KEYS
    per_j = pl.BlockSpec((PEER_HEADS, tt // LANES, nk, LANES), lambda i, e: (0, i, 0, 0))
    per_i = pl.BlockSpec((et // nk, PEER_HEADS, tt), lambda i, e: (e, 0, i))
    return pl.pallas_call(
        _peer_dense_kernel,
        grid=(t // tt, u.shape[0] // et),
        in_specs=[pl.BlockSpec((tt, d), lambda i, e: (i, 0)),
                  pl.BlockSpec((et, d), lambda i, e: (e, 0)),
                  pl.BlockSpec((et, d), lambda i, e: (e, 0)),
                  per_j, per_j, per_i, per_i],
        out_specs=pl.BlockSpec((tt, d), lambda i, e: (i, 0)),
        out_shape=jax.ShapeDtypeStruct((t, d), F32),
        scratch_shapes=[pltpu.VMEM((tt, et), BF16),
                        pltpu.VMEM((PEER_HEADS, tt // LANES, nk, LANES), BF16),
                        pltpu.VMEM((PEER_HEADS, tt // LANES, nk, LANES), BF16)],
        compiler_params=_params(("parallel", "arbitrary")),
        name="peer_dense",
    )(xn, u, v, r2, bj, cnt, ai)


def _final_kernel(h1_ref, peer_ref, p_ref, g3_ref, wg_ref, wp_ref, gf_ref, o_ref, *, last):
    h2 = h1_ref[...] + peer_ref[...]
    xn = _rms(h2, g3_ref[...]).astype(BF16)
    gate = jax.nn.sigmoid(_dot(xn, wg_ref[...]))
    h3 = h2 + gate * _dot(p_ref[...].astype(BF16), wp_ref[...])
    o_ref[...] = _rms(h3, gf_ref[...]) if last else h3


def _final(h1, peer, p2, g3, wg, wp, gf, last, tm=256):
    t, d = h1.shape
    pd = p2.shape[1]
    row = pl.BlockSpec((tm, d), lambda i: (i, 0))
    vec = pl.BlockSpec((1, d), lambda i: (0, 0))
    return pl.pallas_call(
        functools.partial(_final_kernel, last=last),
        grid=(t // tm,),
        in_specs=[row, row, pl.BlockSpec((tm, pd), lambda i: (i, 0)), vec,
                  pl.BlockSpec((d, d), lambda i: (0, 0)),
                  pl.BlockSpec((pd, d), lambda i: (0, 0)), vec],
        out_specs=row,
        out_shape=jax.ShapeDtypeStruct((t, d), F32),
        compiler_params=_params(("parallel",)),
        name="ple_final",
    )(h1, peer, p2, g3, wg, wp, gf)


def kernel(x, p, norm_mix_g, w_in, conv_w, mlstm_gate_b, mlstm_norm_g, w_out, norm_ffn_g,
           peer_wq, peer_keys, peer_u, peer_v, norm_ple_g, ple_w_gate, ple_w_proj, norm_final_g):
    bsz, seq, d = x.shape
    depth = w_in.shape[0]
    t = bsz * seq
    h = x.reshape(t, d)
    for i in range(depth):
        w_main = w_in[i, :, :MAIN_COLS].astype(BF16)
        w_gates = jnp.pad(w_in[i, :, MAIN_COLS:], ((0, 0), (0, LANES - N_GATES))).astype(BF16)
        proj, gates = _rms_proj(h, norm_mix_g[i].reshape(1, d), w_main, w_gates)
        gate_b = mlstm_gate_b[i].astype(F32)
        mix = _mix(proj, gates, gates[:, :N_GATES].T, conv_w[i],
                   jnp.pad(gate_b, (0, LANES - N_GATES)).reshape(1, LANES),
                   gate_b.reshape(N_GATES, 1), mlstm_norm_g[i].reshape(1, MLSTM_DIM), bsz, seq)
        h1 = _matmul_res(mix, w_out[i].astype(BF16), h)
        xn, r2, bj, cnt, ai = _peer_select(h1, norm_ffn_g[i].reshape(1, d),
                                           peer_wq[i].astype(BF16), peer_keys[i].astype(BF16))
        peer = _peer_dense(xn, peer_u[i].astype(BF16), peer_v[i].astype(BF16), r2, bj, cnt, ai)
        h = _final(h1, peer, p[i].reshape(t, -1), norm_ple_g[i].reshape(1, d),
                   ple_w_gate[i].astype(BF16), ple_w_proj[i].astype(BF16),
                   norm_final_g.reshape(1, d), i == depth - 1)
    return h.reshape(bsz, seq, d)
```

```python
import functools
import math

import jax
import jax.numpy as jnp
from jax import lax
from jax.experimental import pallas as pl
from jax.experimental.pallas import tpu as pltpu

F32 = jnp.float32
BF16 = jnp.bfloat16

D_MODEL = 2048
CONV_DIM = 1024
MLSTM_HEADS = 4
MLSTM_HEAD_DIM = 256
MLSTM_DIM = MLSTM_HEADS * MLSTM_HEAD_DIM
MLSTM_CHUNK = 128
MAIN_COLS = 3 * CONV_DIM + 4 * MLSTM_DIM
N_GATES = 2 * MLSTM_HEADS
PEER_HEADS = 8
PEER_N_KEYS = 128
PEER_HALF = 128
PEER_TOPK = 16
EPS = 1e-6

LANES = 128
VMEM_LIMIT = 60 * 1024 * 1024


def _params(semantics):
    return pltpu.CompilerParams(dimension_semantics=semantics, vmem_limit_bytes=VMEM_LIMIT)


def _rms(x, g):
    r = lax.rsqrt(jnp.mean(x * x, axis=-1, keepdims=True) + EPS)
    return (x * r) * g


def _dot(a, b):
    return jnp.dot(a, b, preferred_element_type=F32)


def _dot_nt(a, b):
    return lax.dot_general(a, b, (((1,), (1,)), ((), ())), preferred_element_type=F32)


def _rms_proj_kernel(x_ref, g_ref, w_ref, wg_ref, o_ref, og_ref, xn_ref):
    @pl.when(pl.program_id(1) == 0)
    def _():
        xn = _rms(x_ref[...], g_ref[...]).astype(BF16)
        xn_ref[...] = xn
        og_ref[...] = _dot(xn, wg_ref[...])

    o_ref[...] = _dot(xn_ref[...], w_ref[...])


def _rms_proj(x2, g, w_main, w_gate, tm=1024, tn=1024):
    t, d = x2.shape
    n = w_main.shape[1]
    return pl.pallas_call(
        _rms_proj_kernel,
        grid=(t // tm, n // tn),
        in_specs=[
            pl.BlockSpec((tm, d), lambda i, j: (i, 0)),
            pl.BlockSpec((1, d), lambda i, j: (0, 0)),
            pl.BlockSpec((d, tn), lambda i, j: (0, j)),
            pl.BlockSpec((d, LANES), lambda i, j: (0, 0)),
        ],
        out_specs=[
            pl.BlockSpec((tm, tn), lambda i, j: (i, j)),
            pl.BlockSpec((tm, LANES), lambda i, j: (i, 0)),
        ],
        out_shape=[
            jax.ShapeDtypeStruct((t, n), F32),
            jax.ShapeDtypeStruct((t, LANES), F32),
        ],
        scratch_shapes=[pltpu.VMEM((tm, d), BF16)],
        compiler_params=_params(("parallel", "arbitrary")),
        name="rms_proj",
    )(x2, g, w_main, w_gate)


def _log_sigmoid(x):
    return jnp.minimum(x, 0.0) - jnp.log1p(jnp.exp(-jnp.abs(x)))


def _mix_kernel(bc_ref, cc_ref, uc_ref, q_ref, k_ref, v_ref, o_ref, gc_ref, gr_ref,
                convw_ref, gbc_ref, gbr_ref, ng_ref, y_ref, caug_ref, m_ref, zc_ref):
    L = MLSTM_CHUNK
    hd = MLSTM_HEAD_DIM

    @pl.when(pl.program_id(1) == 0)
    def _():
        caug_ref[...] = jnp.zeros_like(caug_ref)
        m_ref[...] = jnp.zeros_like(m_ref)
        zc_ref[...] = jnp.zeros_like(zc_ref)

    z = cc_ref[...] * uc_ref[...]
    row = lax.broadcasted_iota(jnp.int32, z.shape, 0)
    prev = zc_ref[...]
    p1 = prev[7:8, :]
    p2 = prev[6:7, :]
    z1 = jnp.where(row == 0, p1, pltpu.roll(z, 1, 0))
    z2 = jnp.where(row == 0, p2, jnp.where(row == 1, p1, pltpu.roll(z, 2, 0)))
    cw = convw_ref[...]
    y_conv = bc_ref[...] * (cw[0:1, :] * z2 + cw[1:2, :] * z1 + cw[2:3, :] * z)
    zc_ref[...] = z[L - 8:, :]
    y_ref[:, :CONV_DIM] = y_conv.astype(y_ref.dtype)

    gcol = gc_ref[...] + gbc_ref[...]
    grow = gr_ref[...] + gbr_ref[...]
    ri = lax.broadcasted_iota(jnp.int32, (L, L), 0)
    ci = lax.broadcasted_iota(jnp.int32, (L, L), 1)
    causal = ri >= ci
    tril = causal.astype(F32)
    triu = (ri <= ci).astype(F32)
    hi = lax.Precision.HIGHEST
    b_col_all = jnp.dot(tril, _log_sigmoid(gcol), precision=hi, preferred_element_type=F32)
    b_row_all = jnp.dot(_log_sigmoid(grow), triu, precision=hi, preferred_element_type=F32)
    lane = lax.broadcasted_iota(jnp.int32, (L, LANES), 1)
    ones_col = (lane == 0).astype(BF16)
    scale = hd ** -0.5

    for h in range(MLSTM_HEADS):
        cs = slice(h * hd, (h + 1) * hd)
        qh = q_ref[:, cs].astype(BF16)
        kh = k_ref[:, cs]
        vaug = jnp.concatenate([v_ref[:, cs].astype(BF16), ones_col], axis=1)
        i_row = grow[h:h + 1, :]
        i_col = gcol[:, h:h + 1]
        b_row = b_row_all[MLSTM_HEADS + h:MLSTM_HEADS + h + 1, :]
        b_col = b_col_all[:, MLSTM_HEADS + h:MLSTM_HEADS + h + 1]
        m_prev = m_ref[h:h + 1, 0:1]

        dmat = jnp.where(causal, b_col - b_row + i_row, -jnp.inf)
        inter = b_col + m_prev
        m_t = jnp.maximum(inter, jnp.max(dmat, axis=-1, keepdims=True))
        pw = jnp.exp(dmat - m_t)
        sc = (_dot_nt(qh, kh.astype(BF16)) * scale) * pw
        a = jnp.exp(inter - m_t)
        caug = caug_ref[h]
        nd = a * _dot(qh, caug.astype(BF16)) + _dot(sc.astype(BF16), vaug)
        num = nd[:, :hd]
        den = nd[:, hd:hd + 1]
        hh = num / jnp.maximum(jnp.abs(den), jnp.exp(-m_t))
        hn = _rms(hh, ng_ref[:, cs])
        y_ref[:, CONV_DIM + h * hd:CONV_DIM + (h + 1) * hd] = (
            jax.nn.sigmoid(o_ref[:, cs]) * hn).astype(y_ref.dtype)

        m_new = m_t[L - 1:L, :]
        b_last = b_col[L - 1:L, :]
        w_s = jnp.exp(b_last - b_col + i_col - m_new)
        a_l = jnp.exp(b_last + m_prev - m_new)
        kw = (kh * (w_s * scale)).astype(BF16)
        caug_ref[h] = a_l * caug + _dot(kw.T, vaug)
        m_ref[h:h + 1, :] = jnp.broadcast_to(m_new, (1, LANES))


def _mix(proj, gates_col, gates_row, conv_w, gate_b_col, gate_b_row, norm_g, batch, seq):
    L = MLSTM_CHUNK
    nc = seq // L
    t = batch * seq
    blk = CONV_DIM

    def slab(k):
        return pl.BlockSpec((L, blk), lambda b, c, k=k: (b * nc + c, k))

    return pl.pallas_call(
        _mix_kernel,
        grid=(batch, nc),
        in_specs=[slab(0), slab(1), slab(2), slab(3), slab(4), slab(5), slab(6),
                  pl.BlockSpec((L, LANES), lambda b, c: (b * nc + c, 0)),
                  pl.BlockSpec((N_GATES, L), lambda b, c: (0, b * nc + c)),
                  pl.BlockSpec((3, CONV_DIM), lambda b, c: (0, 0)),
                  pl.BlockSpec((1, LANES), lambda b, c: (0, 0)),
                  pl.BlockSpec((N_GATES, 1), lambda b, c: (0, 0)),
                  pl.BlockSpec((1, MLSTM_DIM), lambda b, c: (0, 0))],
        out_specs=pl.BlockSpec((L, CONV_DIM + MLSTM_DIM), lambda b, c: (b * nc + c, 0)),
        out_shape=jax.ShapeDtypeStruct((t, CONV_DIM + MLSTM_DIM), BF16),
        scratch_shapes=[pltpu.VMEM((MLSTM_HEADS, MLSTM_HEAD_DIM, MLSTM_HEAD_DIM + LANES), F32),
                        pltpu.VMEM((8, LANES), F32),
                        pltpu.VMEM((8, CONV_DIM), F32)],
        compiler_params=_params(("parallel", "arbitrary")),
        name="mix",
    )(proj, proj, proj, proj, proj, proj, proj, gates_col, gates_row, conv_w,
      gate_b_col, gate_b_row, norm_g)


def _matmul_res_kernel(a_ref, w_ref, r_ref, o_ref):
    o_ref[...] = r_ref[...] + _dot(a_ref[...], w_ref[...])


def _matmul_res(a, w, res, tm=512):
    t, k = a.shape
    n = w.shape[1]
    return pl.pallas_call(
        _matmul_res_kernel,
        grid=(t // tm,),
        in_specs=[pl.BlockSpec((tm, k), lambda i: (i, 0)),
                  pl.BlockSpec((k, n), lambda i: (0, 0)),
                  pl.BlockSpec((tm, n), lambda i: (i, 0))],
        out_specs=pl.BlockSpec((tm, n), lambda i: (i, 0)),
        out_shape=jax.ShapeDtypeStruct((t, n), F32),
        compiler_params=_params(("parallel",)),
        name="matmul_res",
    )(a, w, res)


def _topk_rank(s, k):
    n = s.shape[0]
    iota = lax.broadcasted_iota(jnp.int32, s.shape, 0).astype(F32)
    rank = jnp.full(s.shape, float(k), F32)
    vals = []
    for r in range(k):
        mx = jnp.max(s, axis=0, keepdims=True)
        idx = jnp.min(jnp.where(s == mx, iota, float(n)), axis=0, keepdims=True)
        sel = iota == idx
        rank = jnp.where(sel, float(r), rank)
        s = jnp.where(sel, -jnp.inf, s)
        vals.append(mx)
    return rank, vals


def _route_exact(s1, s2):
    K = PEER_TOPK
    rank1, a_vals = _topk_rank(s1, K)
    rank2, b_vals = _topk_rank(s2, K)
    bmat = jnp.concatenate(b_vals, axis=0)
    cmat = jnp.concatenate([a_vals[x] + bmat for x in range(K)], axis=0)
    rank_c, c_vals = _topk_rank(cmat, K)
    chosen = (rank_c < float(K)).astype(F32)
    z = jnp.zeros_like(c_vals[0])
    for v in c_vals:
        z = z + jnp.exp(v - c_vals[0])
    cnt_i = jnp.zeros_like(s1)
    for x in range(K):
        cnt_x = jnp.sum(chosen[x * K:(x + 1) * K, :], axis=0, keepdims=True)
        cnt_i = cnt_i + jnp.where(rank1 == float(x), cnt_x, 0.0)
    return rank2, cnt_i, jnp.exp(s1 - a_vals[0]) / z, jnp.exp(s2 - b_vals[0])


def _sort16_network():
    def merge(lo, hi, r):
        step = r * 2
        if step < hi - lo:
            yield from merge(lo, hi, step)
            yield from merge(lo + r, hi, step)
            yield from [(i, i + r) for i in range(lo + r, hi - r, step)]
        else:
            yield (lo, lo + r)

    def sort(lo, hi):
        if hi - lo >= 1:
            mid = lo + (hi - lo) // 2
            yield from sort(lo, mid)
            yield from sort(mid + 1, hi)
            yield from merge(lo, hi, 1)

    return tuple(sort(0, 15))


_SORT16 = _sort16_network()
SUBLANES = 8


def _top16_sorted(tiles):
    v = list(tiles) + [None] * (16 - len(tiles))
    for i, j in _SORT16:
        a, b = v[i], v[j]
        if b is None:
            continue
        if a is None:
            v[i], v[j] = b, None
        else:
            v[i], v[j] = jnp.maximum(a, b), jnp.minimum(a, b)
    neg = jnp.full(tiles[0].shape, -jnp.inf, F32)
    v = [neg if x is None else x for x in v]
    for shift in (4, 2, 1):
        r = [pltpu.roll(x, shift, 0) for x in v]
        v = [jnp.maximum(v[k], r[15 - k]) for k in range(16)]
        d = 8
        while d >= 1:
            for k in range(16):
                if (k & d) == 0:
                    v[k], v[k + d] = jnp.maximum(v[k], v[k + d]), jnp.minimum(v[k], v[k + d])
            d //= 2
    return v


def _sorted_count(b, pred):
    c8 = pred(b[7])
    c4 = pred(jnp.where(c8, b[11], b[3]))
    c2 = pred(jnp.where(c8, jnp.where(c4, b[13], b[9]), jnp.where(c4, b[5], b[1])))
    hi = jnp.where(c4, jnp.where(c2, b[14], b[12]), jnp.where(c2, b[10], b[8]))
    lo = jnp.where(c4, jnp.where(c2, b[6], b[4]), jnp.where(c2, b[2], b[0]))
    c1 = pred(jnp.where(c8, hi, lo))
    n = (jnp.where(c8, 8.0, 0.0) + jnp.where(c4, 4.0, 0.0)) + (jnp.where(c2, 2.0, 0.0) + jnp.where(c1, 1.0, 0.0))
    return n + jnp.where(pred(b[15]), 1.0, 0.0)


def _route_sorted(s1, s2):
    K = PEER_TOPK
    nk, n = s1.shape
    nt = nk // SUBLANES
    s1t = s1.reshape(nt, SUBLANES, n)
    s2t = s2.reshape(nt, SUBLANES, n)
    a = _top16_sorted([s1t[k] for k in range(nt)])
    b = _top16_sorted([s2t[k] for k in range(nt)])
    sub = lax.broadcasted_iota(jnp.int32, (SUBLANES, n), 0)

    def pack(vals):
        out = vals[0]
        for r in range(1, SUBLANES):
            out = jnp.where(sub == r, vals[r], out)
        return out

    a_lo, a_hi, b_hi = pack(a[:8]), pack(a[8:]), pack(b[8:])
    cands = [a_lo + b[0], a_hi + b[0]]
    for y in range(1, 8):
        cands.append(jnp.where(sub <= K // (y + 1) - 1, a_lo + b[y], -jnp.inf))
    cands.append(a[0] + b_hi)
    cs = _top16_sorted(cands)
    tau = cs[K - 1]
    z = jnp.zeros_like(tau)
    for c in cs:
        z = z + jnp.exp(c - cs[0])
    n3 = jnp.zeros_like(tau)
    for c in cands:
        n3 = n3 + jnp.where(c >= tau, 1.0, 0.0)
    n3 = jnp.sum(n3, axis=0, keepdims=True)

    in1 = s1t >= a[K - 1]
    in2 = s2t >= b[K - 1]
    n1 = jnp.sum(jnp.sum(jnp.where(in1, 1.0, 0.0), axis=0), axis=0, keepdims=True)
    n2 = jnp.sum(jnp.sum(jnp.where(in2, 1.0, 0.0), axis=0), axis=0, keepdims=True)
    bad = jnp.where((n1 != float(K)) | (n2 != float(K)) | (n3 != float(K)), 1.0, 0.0)

    cnt_i = _sorted_count(b, lambda p: s1t + p >= tau)
    r2 = _sorted_count(b, lambda p: p > s2t)
    cnt_i = jnp.where(in1, cnt_i, 0.0)
    ai = jnp.exp(s1t - a[0]) * (1.0 / z)
    bj = jnp.exp(s2t - b[0])
    return (r2.reshape(nk, n), cnt_i.reshape(nk, n), ai.reshape(nk, n), bj.reshape(nk, n), bad)


def _peer_select_kernel(h_ref, g_ref, wq_ref, keys_ref, xn_ref, r2_ref, bj_ref, cnt_ref, ai_ref,
                        q_ref):
    hidx = pl.program_id(1)

    @pl.when(hidx == 0)
    def _():
        xn = _rms(h_ref[...], g_ref[...]).astype(BF16)
        xn_ref[...] = xn
        q = _dot(xn, wq_ref[...]).astype(BF16)
        for g in range(2 * PEER_HEADS):
            q_ref[g] = q[:, g * PEER_HALF:(g + 1) * PEER_HALF]

    s1 = _dot_nt(keys_ref[0, 0], q_ref[2 * hidx])
    s2 = _dot_nt(keys_ref[0, 1], q_ref[2 * hidx + 1])

    def emit(r2, cnt_i, ai, bj):
        r2_ref[0] = r2.astype(BF16)
        bj_ref[0] = bj.astype(BF16)
        for k in range(cnt_i.shape[0] // SUBLANES):
            rs = slice(k * SUBLANES, (k + 1) * SUBLANES)
            cnt_ref[pl.ds(k * SUBLANES, SUBLANES), hidx, :] = cnt_i[rs]
            ai_ref[pl.ds(k * SUBLANES, SUBLANES), hidx, :] = ai[rs]

    r2, cnt_i, ai, bj, bad = _route_sorted(s1, s2)
    emit(r2, cnt_i, ai, bj)

    @pl.when(jnp.max(bad) > 0.0)
    def _():
        emit(*_route_exact(s1, s2))


def _peer_select(h1, g, wq, keys, tt=256):
    t, d = h1.shape
    nk = PEER_N_KEYS
    stat_spec = pl.BlockSpec((nk, PEER_HEADS, tt), lambda i, h: (0, 0, i))
    stat16 = jax.ShapeDtypeStruct((PEER_HEADS, nk, t), BF16)
    stat16_spec = pl.BlockSpec((1, nk, tt), lambda i, h: (h, 0, i))
    return pl.pallas_call(
        _peer_select_kernel,
        grid=(t // tt, PEER_HEADS),
        in_specs=[pl.BlockSpec((tt, d), lambda i, h: (i, 0)),
                  pl.BlockSpec((1, d), lambda i, h: (0, 0)),
                  pl.BlockSpec((d, wq.shape[1]), lambda i, h: (0, 0)),
                  pl.BlockSpec((1, 2, nk, PEER_HALF), lambda i, h: (h, 0, 0, 0))],
        out_specs=[pl.BlockSpec((tt, d), lambda i, h: (i, 0)),
                   stat16_spec, stat16_spec, stat_spec, stat_spec],
        out_shape=[jax.ShapeDtypeStruct((t, d), BF16), stat16, stat16,
                   jax.ShapeDtypeStruct((nk, PEER_HEADS, t), F32),
                   jax.ShapeDtypeStruct((nk, PEER_HEADS, t), F32)],
        scratch_shapes=[pltpu.VMEM((2 * PEER_HEADS, tt, PEER_HALF), BF16)],
        compiler_params=_params(("parallel", "arbitrary")),
        name="peer_select",
    )(h1, g, wq, keys)


def _gelu(a):
    return 0.5 * a * (1.0 + lax.erf(a * (1.0 / math.sqrt(2.0))))


def _peer_dense_kernel(xn_ref, u_ref, v_ref, r2_ref, bj_ref, cnt_ref, ai_ref, o_ref, m_ref):
    @pl.when(pl.program_id(1) == 0)
    def _():
        o_ref[...] = jnp.zeros_like(o_ref)

    tt = xn_ref.shape[0]
    et = u_ref.shape[0]
    nk = PEER_N_KEYS
    rows = 2 * SUBLANES
    tl = 2 * LANES
    zero = jnp.zeros((rows, tl), BF16)
    for il in range(et // nk):
        for tb in range(tt // tl):
            ls = slice(tb * tl, (tb + 1) * tl)
            acc = [None] * (nk // rows)
            for h in range(PEER_HEADS):
                c16 = jnp.broadcast_to(cnt_ref[il, h:h + 1, ls], (rows, tl)).astype(BF16)
                a16 = jnp.broadcast_to(ai_ref[il, h:h + 1, ls], (rows, tl)).astype(BF16)
                for jg in range(nk // rows):
                    js = slice(jg * rows, (jg + 1) * rows)
                    term = jnp.where(r2_ref[h, js, ls] < c16, a16 * bj_ref[h, js, ls], zero)
                    acc[jg] = term if acc[jg] is None else acc[jg] + term
            blk = jnp.concatenate(acc, axis=0).astype(F32)
            m_ref[ls, il * nk:(il + 1) * nk] = blk.T.astype(BF16)
    a = _dot_nt(xn_ref[...], u_ref[...])
    w = _gelu(a).astype(BF16) * m_ref[...]
    o_ref[...] += _dot(w, v_ref[...])


def _peer_dense(xn, u, v, r2, bj, cnt, ai, tt=1024, et=512):
    t, d = xn.shape
    ne = u.shape[0]
    nk = PEER_N_KEYS
    per_j = pl.BlockSpec((PEER_HEADS, nk, tt), lambda i, e: (0, 0, i))
    per_i = pl.BlockSpec((et // nk, PEER_HEADS, tt), lambda i, e: (e, 0, i))
    return pl.pallas_call(
        _peer_dense_kernel,
        grid=(t // tt, ne // et),
        in_specs=[pl.BlockSpec((tt, d), lambda i, e: (i, 0)),
                  pl.BlockSpec((et, d), lambda i, e: (e, 0)),
                  pl.BlockSpec((et, d), lambda i, e: (e, 0)),
                  per_j, per_j, per_i, per_i],
        out_specs=pl.BlockSpec((tt, d), lambda i, e: (i, 0)),
        out_shape=jax.ShapeDtypeStruct((t, d), F32),
        scratch_shapes=[pltpu.VMEM((tt, et), BF16)],
        compiler_params=_params(("parallel", "arbitrary")),
        name="peer_dense",
    )(xn, u, v, r2, bj, cnt, ai)


def _final_kernel(h1_ref, peer_ref, p_ref, g3_ref, wg_ref, wp_ref, gf_ref, o_ref, *, last):
    h2 = h1_ref[...] + peer_ref[...]
    xn = _rms(h2, g3_ref[...]).astype(BF16)
    gate = jax.nn.sigmoid(_dot(xn, wg_ref[...]))
    h3 = h2 + gate * _dot(p_ref[...].astype(BF16), wp_ref[...])
    o_ref[...] = _rms(h3, gf_ref[...]) if last else h3


def _final(h1, peer, p2, g3, wg, wp, gf, last, tm=256):
    t, d = h1.shape
    pd = p2.shape[1]
    row = pl.BlockSpec((tm, d), lambda i: (i, 0))
    vec = pl.BlockSpec((1, d), lambda i: (0, 0))
    return pl.pallas_call(
        functools.partial(_final_kernel, last=last),
        grid=(t // tm,),
        in_specs=[row, row, pl.BlockSpec((tm, pd), lambda i: (i, 0)), vec,
                  pl.BlockSpec((d, d), lambda i: (0, 0)),
                  pl.BlockSpec((pd, d), lambda i: (0, 0)), vec],
        out_specs=row,
        out_shape=jax.ShapeDtypeStruct((t, d), F32),
        compiler_params=_params(("parallel",)),
        name="ple_final",
    )(h1, peer, p2, g3, wg, wp, gf)


def kernel(x, p, norm_mix_g, w_in, conv_w, mlstm_gate_b, mlstm_norm_g, w_out, norm_ffn_g,
           peer_wq, peer_keys, peer_u, peer_v, norm_ple_g, ple_w_gate, ple_w_proj, norm_final_g):
    bsz, seq, d = x.shape
    depth = w_in.shape[0]
    t = bsz * seq
    h = x.reshape(t, d)
    for i in range(depth):
        w_main = w_in[i, :, :MAIN_COLS].astype(BF16)
        w_gates = jnp.pad(w_in[i, :, MAIN_COLS:], ((0, 0), (0, LANES - N_GATES))).astype(BF16)
        proj, gates = _rms_proj(h, norm_mix_g[i].reshape(1, d), w_main, w_gates)
        gate_b = mlstm_gate_b[i].astype(F32)
        mix = _mix(proj, gates, gates[:, :N_GATES].T, conv_w[i],
                   jnp.pad(gate_b, (0, LANES - N_GATES)).reshape(1, LANES),
                   gate_b.reshape(N_GATES, 1), mlstm_norm_g[i].reshape(1, MLSTM_DIM), bsz, seq)
        h1 = _matmul_res(mix, w_out[i].astype(BF16), h)
        xn, r2, bj, cnt, ai = _peer_select(h1, norm_ffn_g[i].reshape(1, d),
                                           peer_wq[i].astype(BF16), peer_keys[i].astype(BF16))
        peer = _peer_dense(xn, peer_u[i].astype(BF16), peer_v[i].astype(BF16), r2, bj, cnt, ai)
        h = _final(h1, peer, p[i].reshape(t, -1), norm_ple_g[i].reshape(1, d),
                   ple_w_gate[i].astype(BF16), ple_w_proj[i].astype(BF16),
                   norm_final_g.reshape(1, d), i == depth - 1)
    return h.reshape(bsz, seq, d)
```

```python
import functools
import math

import jax
import jax.numpy as jnp
from jax import lax
from jax.experimental import pallas as pl
from jax.experimental.pallas import tpu as pltpu

F32 = jnp.float32
BF16 = jnp.bfloat16

D_MODEL = 2048
CONV_DIM = 1024
MLSTM_HEADS = 4
MLSTM_HEAD_DIM = 256
MLSTM_DIM = MLSTM_HEADS * MLSTM_HEAD_DIM
MLSTM_CHUNK = 128
MAIN_COLS = 3 * CONV_DIM + 4 * MLSTM_DIM
N_GATES = 2 * MLSTM_HEADS
PEER_HEADS = 8
PEER_N_KEYS = 128
PEER_HALF = 128
PEER_TOPK = 16
EPS = 1e-6

LANES = 128
VMEM_LIMIT = 60 * 1024 * 1024


def _params(semantics):
    return pltpu.CompilerParams(dimension_semantics=semantics, vmem_limit_bytes=VMEM_LIMIT)


def _rms(x, g):
    r = lax.rsqrt(jnp.mean(x * x, axis=-1, keepdims=True) + EPS)
    return (x * r) * g


def _dot(a, b):
    return jnp.dot(a, b, preferred_element_type=F32)


def _dot_nt(a, b):
    return lax.dot_general(a, b, (((1,), (1,)), ((), ())), preferred_element_type=F32)


def _rms_proj_kernel(x_ref, g_ref, w_ref, wg_ref, o_ref, og_ref, xn_ref):
    @pl.when(pl.program_id(1) == 0)
    def _():
        xn = _rms(x_ref[...], g_ref[...]).astype(BF16)
        xn_ref[...] = xn
        og_ref[...] = _dot(xn, wg_ref[...])

    o_ref[...] = _dot(xn_ref[...], w_ref[...])


def _rms_proj(x2, g, w_all, w_gate, tm=1024, tn=1024):
    t, d = x2.shape
    n = MAIN_COLS
    return pl.pallas_call(
        _rms_proj_kernel,
        grid=(t // tm, n // tn),
        in_specs=[
            pl.BlockSpec((tm, d), lambda i, j: (i, 0)),
            pl.BlockSpec((1, d), lambda i, j: (0, 0)),
            pl.BlockSpec((d, tn), lambda i, j: (0, j)),
            pl.BlockSpec((d, LANES), lambda i, j: (0, 0)),
        ],
        out_specs=[
            pl.BlockSpec((tm, tn), lambda i, j: (i, j)),
            pl.BlockSpec((tm, LANES), lambda i, j: (i, 0)),
        ],
        out_shape=[
            jax.ShapeDtypeStruct((t, n), F32),
            jax.ShapeDtypeStruct((t, LANES), F32),
        ],
        scratch_shapes=[pltpu.VMEM((tm, d), BF16)],
        compiler_params=_params(("parallel", "arbitrary")),
        name="rms_proj",
    )(x2, g, w_all, w_gate)


def _log_sigmoid(x):
    return jnp.minimum(x, 0.0) - jnp.log1p(jnp.exp(-jnp.abs(x)))


def _mix_kernel(bc_ref, cc_ref, uc_ref, q_ref, k_ref, v_ref, o_ref, gc_ref, gr_ref,
                convw_ref, gbc_ref, gbr_ref, ng_ref, tu_ref, tv_ref, y_ref, tub_ref, tvb_ref,
                caug_ref, m_ref, zc_ref):
    L = MLSTM_CHUNK
    hd = MLSTM_HEAD_DIM

    tub_ref[...] = tu_ref[...].astype(BF16)
    tvb_ref[...] = tv_ref[...].astype(BF16)

    @pl.when(pl.program_id(1) == 0)
    def _():
        caug_ref[...] = jnp.zeros_like(caug_ref)
        m_ref[...] = jnp.zeros_like(m_ref)
        zc_ref[...] = jnp.zeros_like(zc_ref)

    z = cc_ref[...] * uc_ref[...]
    row = lax.broadcasted_iota(jnp.int32, z.shape, 0)
    prev = zc_ref[...]
    p1 = prev[7:8, :]
    p2 = prev[6:7, :]
    z1 = jnp.where(row == 0, p1, pltpu.roll(z, 1, 0))
    z2 = jnp.where(row == 0, p2, jnp.where(row == 1, p1, pltpu.roll(z, 2, 0)))
    cw = convw_ref[...]
    y_conv = bc_ref[...] * (cw[0:1, :] * z2 + cw[1:2, :] * z1 + cw[2:3, :] * z)
    zc_ref[...] = z[L - 8:, :]
    y_ref[:, :CONV_DIM] = y_conv.astype(y_ref.dtype)

    gcol = gc_ref[...] + gbc_ref[...]
    grow = gr_ref[...] + gbr_ref[...]
    ri = lax.broadcasted_iota(jnp.int32, (L, L), 0)
    ci = lax.broadcasted_iota(jnp.int32, (L, L), 1)
    causal = ri >= ci
    tril = causal.astype(F32)
    triu = (ri <= ci).astype(F32)
    hi = lax.Precision.HIGHEST
    b_col_all = jnp.dot(tril, _log_sigmoid(gcol), precision=hi, preferred_element_type=F32)
    b_row_all = jnp.dot(_log_sigmoid(grow), triu, precision=hi, preferred_element_type=F32)
    lane = lax.broadcasted_iota(jnp.int32, (L, LANES), 1)
    ones_col = (lane == 0).astype(BF16)
    scale = hd ** -0.5

    for h in range(MLSTM_HEADS):
        cs = slice(h * hd, (h + 1) * hd)
        qh = q_ref[:, cs].astype(BF16)
        kh = k_ref[:, cs]
        vaug = jnp.concatenate([v_ref[:, cs].astype(BF16), ones_col], axis=1)
        i_row = grow[h:h + 1, :]
        i_col = gcol[:, h:h + 1]
        b_row = b_row_all[MLSTM_HEADS + h:MLSTM_HEADS + h + 1, :]
        b_col = b_col_all[:, MLSTM_HEADS + h:MLSTM_HEADS + h + 1]
        m_prev = m_ref[h:h + 1, 0:1]

        dmat = jnp.where(causal, b_col - b_row + i_row, -jnp.inf)
        inter = b_col + m_prev
        m_t = jnp.maximum(inter, jnp.max(dmat, axis=-1, keepdims=True))
        pw = jnp.exp(dmat - m_t)
        sc = (_dot_nt(qh, kh.astype(BF16)) * scale) * pw
        a = jnp.exp(inter - m_t)
        caug = caug_ref[h]
        nd = a * _dot(qh, caug.astype(BF16)) + _dot(sc.astype(BF16), vaug)
        num = nd[:, :hd]
        den = nd[:, hd:hd + 1]
        hh = num / jnp.maximum(jnp.abs(den), jnp.exp(-m_t))
        hn = _rms(hh, ng_ref[:, cs])
        y_ref[:, CONV_DIM + h * hd:CONV_DIM + (h + 1) * hd] = (
            jax.nn.sigmoid(o_ref[:, cs]) * hn).astype(y_ref.dtype)

        m_new = m_t[L - 1:L, :]
        b_last = b_col[L - 1:L, :]
        w_s = jnp.exp(b_last - b_col + i_col - m_new)
        a_l = jnp.exp(b_last + m_prev - m_new)
        kw = (kh * (w_s * scale)).astype(BF16)
        caug_ref[h] = a_l * caug + _dot(kw.T, vaug)
        m_ref[h:h + 1, :] = jnp.broadcast_to(m_new, (1, LANES))


def _mix(proj, gates_col, gates_row, conv_w, gate_b_col, gate_b_row, norm_g, tab_u, tab_v,
         batch, seq):
    L = MLSTM_CHUNK
    nc = seq // L
    t = batch * seq
    blk = CONV_DIM
    n_rows, width = tab_u.shape
    tr = n_rows // (batch * nc)
    tab_spec = pl.BlockSpec((tr, width), lambda b, c: (b * nc + c, 0))

    def slab(k):
        return pl.BlockSpec((L, blk), lambda b, c, k=k: (b * nc + c, k))

    return pl.pallas_call(
        _mix_kernel,
        grid=(batch, nc),
        in_specs=[slab(0), slab(1), slab(2), slab(3), slab(4), slab(5), slab(6),
                  pl.BlockSpec((L, LANES), lambda b, c: (b * nc + c, 0)),
                  pl.BlockSpec((N_GATES, L), lambda b, c: (0, b * nc + c)),
                  pl.BlockSpec((3, CONV_DIM), lambda b, c: (0, 0)),
                  pl.BlockSpec((1, LANES), lambda b, c: (0, 0)),
                  pl.BlockSpec((N_GATES, 1), lambda b, c: (0, 0)),
                  pl.BlockSpec((1, MLSTM_DIM), lambda b, c: (0, 0)),
                  tab_spec, tab_spec],
        out_specs=[pl.BlockSpec((L, CONV_DIM + MLSTM_DIM), lambda b, c: (b * nc + c, 0)),
                   tab_spec, tab_spec],
        out_shape=[jax.ShapeDtypeStruct((t, CONV_DIM + MLSTM_DIM), BF16),
                   jax.ShapeDtypeStruct((n_rows, width), BF16),
                   jax.ShapeDtypeStruct((n_rows, width), BF16)],
        scratch_shapes=[pltpu.VMEM((MLSTM_HEADS, MLSTM_HEAD_DIM, MLSTM_HEAD_DIM + LANES), F32),
                        pltpu.VMEM((8, LANES), F32),
                        pltpu.VMEM((8, CONV_DIM), F32)],
        compiler_params=_params(("parallel", "arbitrary")),
        name="mix",
    )(proj, proj, proj, proj, proj, proj, proj, gates_col, gates_row, conv_w,
      gate_b_col, gate_b_row, norm_g, tab_u, tab_v)


def _matmul_res_kernel(a_ref, w_ref, r_ref, o_ref):
    o_ref[...] = r_ref[...] + _dot(a_ref[...], w_ref[...])


def _matmul_res(a, w, res, tm=512):
    t, k = a.shape
    n = w.shape[1]
    return pl.pallas_call(
        _matmul_res_kernel,
        grid=(t // tm,),
        in_specs=[pl.BlockSpec((tm, k), lambda i: (i, 0)),
                  pl.BlockSpec((k, n), lambda i: (0, 0)),
                  pl.BlockSpec((tm, n), lambda i: (i, 0))],
        out_specs=pl.BlockSpec((tm, n), lambda i: (i, 0)),
        out_shape=jax.ShapeDtypeStruct((t, n), F32),
        compiler_params=_params(("parallel",)),
        name="matmul_res",
    )(a, w, res)


def _topk_rank(s, k):
    n = s.shape[0]
    iota = lax.broadcasted_iota(jnp.int32, s.shape, 0).astype(F32)
    rank = jnp.full(s.shape, float(k), F32)
    vals = []
    for r in range(k):
        mx = jnp.max(s, axis=0, keepdims=True)
        idx = jnp.min(jnp.where(s == mx, iota, float(n)), axis=0, keepdims=True)
        sel = iota == idx
        rank = jnp.where(sel, float(r), rank)
        s = jnp.where(sel, -jnp.inf, s)
        vals.append(mx)
    return rank, vals


def _route_exact(s1, s2):
    K = PEER_TOPK
    rank1, a_vals = _topk_rank(s1, K)
    rank2, b_vals = _topk_rank(s2, K)
    bmat = jnp.concatenate(b_vals, axis=0)
    cmat = jnp.concatenate([a_vals[x] + bmat for x in range(K)], axis=0)
    rank_c, c_vals = _topk_rank(cmat, K)
    chosen = (rank_c < float(K)).astype(F32)
    z = jnp.zeros_like(c_vals[0])
    for v in c_vals:
        z = z + jnp.exp(v - c_vals[0])
    cnt_i = jnp.zeros_like(s1)
    for x in range(K):
        cnt_x = jnp.sum(chosen[x * K:(x + 1) * K, :], axis=0, keepdims=True)
        cnt_i = cnt_i + jnp.where(rank1 == float(x), cnt_x, 0.0)
    return rank2, cnt_i, jnp.exp(s1 - a_vals[0]) / z, jnp.exp(s2 - b_vals[0])


def _sort16_network():
    def merge(lo, hi, r):
        step = r * 2
        if step < hi - lo:
            yield from merge(lo, hi, step)
            yield from merge(lo + r, hi, step)
            yield from [(i, i + r) for i in range(lo + r, hi - r, step)]
        else:
            yield (lo, lo + r)

    def sort(lo, hi):
        if hi - lo >= 1:
            mid = lo + (hi - lo) // 2
            yield from sort(lo, mid)
            yield from sort(mid + 1, hi)
            yield from merge(lo, hi, 1)

    return tuple(sort(0, 15))


_SORT16 = _sort16_network()
SUBLANES = 8


def _top16_sorted(tiles):
    v = list(tiles) + [None] * (16 - len(tiles))
    for i, j in _SORT16:
        a, b = v[i], v[j]
        if b is None:
            continue
        if a is None:
            v[i], v[j] = b, None
        else:
            v[i], v[j] = jnp.maximum(a, b), jnp.minimum(a, b)
    neg = jnp.full(tiles[0].shape, -jnp.inf, F32)
    v = [neg if x is None else x for x in v]
    for shift in (4, 2, 1):
        r = [pltpu.roll(x, shift, 0) for x in v]
        v = [jnp.maximum(v[k], r[15 - k]) for k in range(16)]
        d = 8
        while d >= 1:
            for k in range(16):
                if (k & d) == 0:
                    v[k], v[k + d] = jnp.maximum(v[k], v[k + d]), jnp.minimum(v[k], v[k + d])
            d //= 2
    return v


def _sorted_count(b, pred):
    c8 = pred(b[7])
    c4 = pred(jnp.where(c8, b[11], b[3]))
    c2 = pred(jnp.where(c8, jnp.where(c4, b[13], b[9]), jnp.where(c4, b[5], b[1])))
    hi = jnp.where(c4, jnp.where(c2, b[14], b[12]), jnp.where(c2, b[10], b[8]))
    lo = jnp.where(c4, jnp.where(c2, b[6], b[4]), jnp.where(c2, b[2], b[0]))
    c1 = pred(jnp.where(c8, hi, lo))
    n = (jnp.where(c8, 8.0, 0.0) + jnp.where(c4, 4.0, 0.0)) + (jnp.where(c2, 2.0, 0.0) + jnp.where(c1, 1.0, 0.0))
    return n + jnp.where(pred(b[15]), 1.0, 0.0)


def _route_sorted(s1, s2):
    K = PEER_TOPK
    nk, n = s1.shape
    nt = nk // SUBLANES
    s1t = s1.reshape(nt, SUBLANES, n)
    s2t = s2.reshape(nt, SUBLANES, n)
    a = _top16_sorted([s1t[k] for k in range(nt)])
    b = _top16_sorted([s2t[k] for k in range(nt)])
    sub = lax.broadcasted_iota(jnp.int32, (SUBLANES, n), 0)

    def pack(vals):
        out = vals[0]
        for r in range(1, SUBLANES):
            out = jnp.where(sub == r, vals[r], out)
        return out

    a_lo, a_hi, b_hi = pack(a[:8]), pack(a[8:]), pack(b[8:])
    cands = [a_lo + b[0], a_hi + b[0]]
    for y in range(1, 8):
        cands.append(jnp.where(sub <= K // (y + 1) - 1, a_lo + b[y], -jnp.inf))
    cands.append(a[0] + b_hi)
    cs = _top16_sorted(cands)
    tau = cs[K - 1]
    z = jnp.zeros_like(tau)
    for c in cs:
        z = z + jnp.exp(c - cs[0])
    n3 = jnp.zeros_like(tau)
    for c in cands:
        n3 = n3 + jnp.where(c >= tau, 1.0, 0.0)
    n3 = jnp.sum(n3, axis=0, keepdims=True)

    in1 = s1t >= a[K - 1]
    in2 = s2t >= b[K - 1]
    n1 = jnp.sum(jnp.sum(jnp.where(in1, 1.0, 0.0), axis=0), axis=0, keepdims=True)
    n2 = jnp.sum(jnp.sum(jnp.where(in2, 1.0, 0.0), axis=0), axis=0, keepdims=True)
    bad = jnp.where((n1 != float(K)) | (n2 != float(K)) | (n3 != float(K)), 1.0, 0.0)

    cnt_i = _sorted_count(b, lambda p: s1t + p >= tau)
    r2 = _sorted_count(b, lambda p: p > s2t)
    cnt_i = jnp.where(in1, cnt_i, 0.0)
    ai = jnp.exp(s1t - a[0]) * (1.0 / z)
    bj = jnp.exp(s2t - b[0])
    return (r2.reshape(nk, n), cnt_i.reshape(nk, n), ai.reshape(nk, n), bj.reshape(nk, n), bad)


def _peer_select_kernel(h_ref, g_ref, wq_ref, keys_ref, xn_ref, r2_ref, bj_ref, cnt_ref, ai_ref,
                        q_ref):
    hidx = pl.program_id(1)

    @pl.when(hidx == 0)
    def _():
        xn = _rms(h_ref[...], g_ref[...]).astype(BF16)
        xn_ref[...] = xn
        q = _dot(xn, wq_ref[...]).astype(BF16)
        for g in range(2 * PEER_HEADS):
            q_ref[g] = q[:, g * PEER_HALF:(g + 1) * PEER_HALF]

    s1 = _dot_nt(keys_ref[0, 0], q_ref[2 * hidx])
    s2 = _dot_nt(keys_ref[0, 1], q_ref[2 * hidx + 1])

    def emit(r2, cnt_i, ai, bj):
        r2_ref[0] = r2.astype(BF16)
        bj_ref[0] = bj.astype(BF16)
        for k in range(cnt_i.shape[0] // SUBLANES):
            rs = slice(k * SUBLANES, (k + 1) * SUBLANES)
            cnt_ref[pl.ds(k * SUBLANES, SUBLANES), hidx, :] = cnt_i[rs]
            ai_ref[pl.ds(k * SUBLANES, SUBLANES), hidx, :] = ai[rs]

    r2, cnt_i, ai, bj, bad = _route_sorted(s1, s2)
    emit(r2, cnt_i, ai, bj)

    @pl.when(jnp.max(bad) > 0.0)
    def _():
        emit(*_route_exact(s1, s2))


def _peer_select(h1, g, wq, keys, tt=256):
    t, d = h1.shape
    nk = PEER_N_KEYS
    stat_spec = pl.BlockSpec((nk, PEER_HEADS, tt), lambda i, h: (0, 0, i))
    stat16 = jax.ShapeDtypeStruct((PEER_HEADS, nk, t), BF16)
    stat16_spec = pl.BlockSpec((1, nk, tt), lambda i, h: (h, 0, i))
    return pl.pallas_call(
        _peer_select_kernel,
        grid=(t // tt, PEER_HEADS),
        in_specs=[pl.BlockSpec((tt, d), lambda i, h: (i, 0)),
                  pl.BlockSpec((1, d), lambda i, h: (0, 0)),
                  pl.BlockSpec((d, wq.shape[1]), lambda i, h: (0, 0)),
                  pl.BlockSpec((1, 2, nk, PEER_HALF), lambda i, h: (h, 0, 0, 0))],
        out_specs=[pl.BlockSpec((tt, d), lambda i, h: (i, 0)),
                   stat16_spec, stat16_spec, stat_spec, stat_spec],
        out_shape=[jax.ShapeDtypeStruct((t, d), BF16), stat16, stat16,
                   jax.ShapeDtypeStruct((nk, PEER_HEADS, t), F32),
                   jax.ShapeDtypeStruct((nk, PEER_HEADS, t), F32)],
        scratch_shapes=[pltpu.VMEM((2 * PEER_HEADS, tt, PEER_HALF), BF16)],
        compiler_params=_params(("parallel", "arbitrary")),
        name="peer_select",
    )(h1, g, wq, keys)


def _gelu(a):
    return 0.5 * a * (1.0 + lax.erf(a * (1.0 / math.sqrt(2.0))))


def _peer_dense_kernel(xn_ref, u_ref, v_ref, r2_ref, bj_ref, cnt_ref, ai_ref, o_ref, m_ref):
    @pl.when(pl.program_id(1) == 0)
    def _():
        o_ref[...] = jnp.zeros_like(o_ref)

    tt = xn_ref.shape[0]
    et = u_ref.shape[0]
    nk = PEER_N_KEYS
    rows = 2 * SUBLANES
    tl = 2 * LANES
    zero = jnp.zeros((rows, tl), BF16)
    for il in range(et // nk):
        for tb in range(tt // tl):
            ls = slice(tb * tl, (tb + 1) * tl)
            acc = [None] * (nk // rows)
            for h in range(PEER_HEADS):
                c16 = jnp.broadcast_to(cnt_ref[il, h:h + 1, ls], (rows, tl)).astype(BF16)
                a16 = jnp.broadcast_to(ai_ref[il, h:h + 1, ls], (rows, tl)).astype(BF16)
                for jg in range(nk // rows):
                    js = slice(jg * rows, (jg + 1) * rows)
                    term = jnp.where(r2_ref[h, js, ls] < c16, a16 * bj_ref[h, js, ls], zero)
                    acc[jg] = term if acc[jg] is None else acc[jg] + term
            blk = jnp.concatenate(acc, axis=0).astype(F32)
            m_ref[ls, il * nk:(il + 1) * nk] = blk.T.astype(BF16)
    a = _dot_nt(xn_ref[...], u_ref[...])
    w = _gelu(a).astype(BF16) * m_ref[...]
    o_ref[...] += _dot(w, v_ref[...])


def _peer_dense(xn, u, v, r2, bj, cnt, ai, tt=1024, et=512):
    t, d = xn.shape
    ne = u.shape[0]
    nk = PEER_N_KEYS
    per_j = pl.BlockSpec((PEER_HEADS, nk, tt), lambda i, e: (0, 0, i))
    per_i = pl.BlockSpec((et // nk, PEER_HEADS, tt), lambda i, e: (e, 0, i))
    return pl.pallas_call(
        _peer_dense_kernel,
        grid=(t // tt, ne // et),
        in_specs=[pl.BlockSpec((tt, d), lambda i, e: (i, 0)),
                  pl.BlockSpec((et, d), lambda i, e: (e, 0)),
                  pl.BlockSpec((et, d), lambda i, e: (e, 0)),
                  per_j, per_j, per_i, per_i],
        out_specs=pl.BlockSpec((tt, d), lambda i, e: (i, 0)),
        out_shape=jax.ShapeDtypeStruct((t, d), F32),
        scratch_shapes=[pltpu.VMEM((tt, et), BF16)],
        compiler_params=_params(("parallel", "arbitrary")),
        name="peer_dense",
    )(xn, u, v, r2, bj, cnt, ai)


def _final_kernel(h1_ref, peer_ref, p_ref, g3_ref, wg_ref, wp_ref, gf_ref, o_ref, *, last):
    h2 = h1_ref[...] + peer_ref[...]
    xn = _rms(h2, g3_ref[...]).astype(BF16)
    gate = jax.nn.sigmoid(_dot(xn, wg_ref[...]))
    h3 = h2 + gate * _dot(p_ref[...].astype(BF16), wp_ref[...])
    o_ref[...] = _rms(h3, gf_ref[...]) if last else h3


def _final(h1, peer, p2, g3, wg, wp, gf, last, tm=256):
    t, d = h1.shape
    pd = p2.shape[1]
    row = pl.BlockSpec((tm, d), lambda i: (i, 0))
    vec = pl.BlockSpec((1, d), lambda i: (0, 0))
    return pl.pallas_call(
        functools.partial(_final_kernel, last=last),
        grid=(t // tm,),
        in_specs=[row, row, pl.BlockSpec((tm, pd), lambda i: (i, 0)), vec,
                  pl.BlockSpec((d, d), lambda i: (0, 0)),
                  pl.BlockSpec((pd, d), lambda i: (0, 0)), vec],
        out_specs=row,
        out_shape=jax.ShapeDtypeStruct((t, d), F32),
        compiler_params=_params(("parallel",)),
        name="ple_final",
    )(h1, peer, p2, g3, wg, wp, gf)


def kernel(x, p, norm_mix_g, w_in, conv_w, mlstm_gate_b, mlstm_norm_g, w_out, norm_ffn_g,
           peer_wq, peer_keys, peer_u, peer_v, norm_ple_g, ple_w_gate, ple_w_proj, norm_final_g):
    bsz, seq, d = x.shape
    depth = w_in.shape[0]
    t = bsz * seq
    h = x.reshape(t, d)
    for i in range(depth):
        w_all = w_in[i].astype(BF16)
        w_gates = jnp.pad(w_all[:, MAIN_COLS:], ((0, 0), (0, LANES - N_GATES)))
        proj, gates = _rms_proj(h, norm_mix_g[i].reshape(1, d), w_all, w_gates)
        gate_b = mlstm_gate_b[i].astype(F32)
        mix, u_bf, v_bf = _mix(proj, gates, gates[:, :N_GATES].T, conv_w[i],
                               jnp.pad(gate_b, (0, LANES - N_GATES)).reshape(1, LANES),
                               gate_b.reshape(N_GATES, 1), mlstm_norm_g[i].reshape(1, MLSTM_DIM),
                               peer_u[i], peer_v[i], bsz, seq)
        h1 = _matmul_res(mix, w_out[i].astype(BF16), h)
        xn, r2, bj, cnt, ai = _peer_select(h1, norm_ffn_g[i].reshape(1, d),
                                           peer_wq[i].astype(BF16), peer_keys[i].astype(BF16))
        peer = _peer_dense(xn, u_bf, v_bf, r2, bj, cnt, ai)
        h = _final(h1, peer, p[i].reshape(t, -1), norm_ple_g[i].reshape(1, d),
                   ple_w_gate[i].astype(BF16), ple_w_proj[i].astype(BF16),
                   norm_final_g.reshape(1, d), i == depth - 1)
    return h.reshape(bsz, seq, d)
```

```python
import functools
import math

import jax
import jax.numpy as jnp
from jax import lax
from jax.experimental import pallas as pl
from jax.experimental.pallas import tpu as pltpu

F32 = jnp.float32
BF16 = jnp.bfloat16

D_MODEL = 2048
CONV_DIM = 1024
MLSTM_HEADS = 4
MLSTM_HEAD_DIM = 256
MLSTM_DIM = MLSTM_HEADS * MLSTM_HEAD_DIM
MLSTM_CHUNK = 128
MAIN_COLS = 3 * CONV_DIM + 4 * MLSTM_DIM
N_GATES = 2 * MLSTM_HEADS
PEER_HEADS = 8
PEER_N_KEYS = 128
PEER_HALF = 128
PEER_TOPK = 16
EPS = 1e-6

LANES = 128
VMEM_LIMIT = 60 * 1024 * 1024


def _params(semantics):
    return pltpu.CompilerParams(dimension_semantics=semantics, vmem_limit_bytes=VMEM_LIMIT)


def _rms(x, g):
    r = lax.rsqrt(jnp.mean(x * x, axis=-1, keepdims=True) + EPS)
    return (x * r) * g


def _dot(a, b):
    return jnp.dot(a, b, preferred_element_type=F32)


def _dot_nt(a, b):
    return lax.dot_general(a, b, (((1,), (1,)), ((), ())), preferred_element_type=F32)


def _rms_proj_kernel(x_ref, g_ref, w_ref, wg_ref, o_ref, og_ref, xn_ref):
    @pl.when(pl.program_id(1) == 0)
    def _():
        xn = _rms(x_ref[...], g_ref[...]).astype(BF16)
        xn_ref[...] = xn
        og_ref[...] = _dot(xn, wg_ref[...])

    o_ref[...] = _dot(xn_ref[...], w_ref[...])


def _rms_proj(x2, g, w_all, w_gate, tm=1024, tn=1024):
    t, d = x2.shape
    n = MAIN_COLS
    return pl.pallas_call(
        _rms_proj_kernel,
        grid=(t // tm, n // tn),
        in_specs=[
            pl.BlockSpec((tm, d), lambda i, j: (i, 0)),
            pl.BlockSpec((1, d), lambda i, j: (0, 0)),
            pl.BlockSpec((d, tn), lambda i, j: (0, j)),
            pl.BlockSpec((d, LANES), lambda i, j: (0, 0)),
        ],
        out_specs=[
            pl.BlockSpec((tm, tn), lambda i, j: (i, j)),
            pl.BlockSpec((tm, LANES), lambda i, j: (i, 0)),
        ],
        out_shape=[
            jax.ShapeDtypeStruct((t, n), F32),
            jax.ShapeDtypeStruct((t, LANES), F32),
        ],
        scratch_shapes=[pltpu.VMEM((tm, d), BF16)],
        compiler_params=_params(("parallel", "arbitrary")),
        name="rms_proj",
    )(x2, g, w_all, w_gate)


def _log_sigmoid(x):
    return jnp.minimum(x, 0.0) - jnp.log1p(jnp.exp(-jnp.abs(x)))


def _mix_kernel(bc_ref, cc_ref, uc_ref, q_ref, k_ref, v_ref, o_ref, gc_ref, gr_ref,
                convw_ref, gbc_ref, gbr_ref, ng_ref, *rest, n_tab):
    tab_refs = rest[:n_tab]
    y_ref = rest[n_tab]
    tab_out_refs = rest[n_tab + 1:2 * n_tab + 1]
    caug_ref, m_ref, zc_ref = rest[2 * n_tab + 1:]
    L = MLSTM_CHUNK
    hd = MLSTM_HEAD_DIM

    for src, dst in zip(tab_refs, tab_out_refs):
        dst[...] = src[...].astype(BF16)

    @pl.when(pl.program_id(1) == 0)
    def _():
        caug_ref[...] = jnp.zeros_like(caug_ref)
        m_ref[...] = jnp.zeros_like(m_ref)
        zc_ref[...] = jnp.zeros_like(zc_ref)

    z = cc_ref[...] * uc_ref[...]
    row = lax.broadcasted_iota(jnp.int32, z.shape, 0)
    prev = zc_ref[...]
    p1 = prev[7:8, :]
    p2 = prev[6:7, :]
    z1 = jnp.where(row == 0, p1, pltpu.roll(z, 1, 0))
    z2 = jnp.where(row == 0, p2, jnp.where(row == 1, p1, pltpu.roll(z, 2, 0)))
    cw = convw_ref[...]
    y_conv = bc_ref[...] * (cw[0:1, :] * z2 + cw[1:2, :] * z1 + cw[2:3, :] * z)
    zc_ref[...] = z[L - 8:, :]
    y_ref[:, :CONV_DIM] = y_conv.astype(y_ref.dtype)

    gcol = gc_ref[...] + gbc_ref[...]
    grow = gr_ref[...] + gbr_ref[...]
    ri = lax.broadcasted_iota(jnp.int32, (L, L), 0)
    ci = lax.broadcasted_iota(jnp.int32, (L, L), 1)
    causal = ri >= ci
    tril = causal.astype(F32)
    triu = (ri <= ci).astype(F32)
    hi = lax.Precision.HIGHEST
    b_col_all = jnp.dot(tril, _log_sigmoid(gcol), precision=hi, preferred_element_type=F32)
    b_row_all = jnp.dot(_log_sigmoid(grow), triu, precision=hi, preferred_element_type=F32)
    lane = lax.broadcasted_iota(jnp.int32, (L, LANES), 1)
    ones_col = (lane == 0).astype(BF16)
    scale = hd ** -0.5

    for h in range(MLSTM_HEADS):
        cs = slice(h * hd, (h + 1) * hd)
        qh = q_ref[:, cs].astype(BF16)
        kh = k_ref[:, cs]
        vaug = jnp.concatenate([v_ref[:, cs].astype(BF16), ones_col], axis=1)
        i_row = grow[h:h + 1, :]
        i_col = gcol[:, h:h + 1]
        b_row = b_row_all[MLSTM_HEADS + h:MLSTM_HEADS + h + 1, :]
        b_col = b_col_all[:, MLSTM_HEADS + h:MLSTM_HEADS + h + 1]
        m_prev = m_ref[h:h + 1, 0:1]

        dmat = jnp.where(causal, b_col - b_row + i_row, -jnp.inf)
        inter = b_col + m_prev
        m_t = jnp.maximum(inter, jnp.max(dmat, axis=-1, keepdims=True))
        pw = jnp.exp(dmat - m_t)
        sc = (_dot_nt(qh, kh.astype(BF16)) * scale) * pw
        a = jnp.exp(inter - m_t)
        caug = caug_ref[h]
        nd = a * _dot(qh, caug.astype(BF16)) + _dot(sc.astype(BF16), vaug)
        num = nd[:, :hd]
        den = nd[:, hd:hd + 1]
        hh = num / jnp.maximum(jnp.abs(den), jnp.exp(-m_t))
        hn = _rms(hh, ng_ref[:, cs])
        y_ref[:, CONV_DIM + h * hd:CONV_DIM + (h + 1) * hd] = (
            jax.nn.sigmoid(o_ref[:, cs]) * hn).astype(y_ref.dtype)

        m_new = m_t[L - 1:L, :]
        b_last = b_col[L - 1:L, :]
        w_s = jnp.exp(b_last - b_col + i_col - m_new)
        a_l = jnp.exp(b_last + m_prev - m_new)
        kw = (kh * (w_s * scale)).astype(BF16)
        caug_ref[h] = a_l * caug + _dot(kw.T, vaug)
        m_ref[h:h + 1, :] = jnp.broadcast_to(m_new, (1, LANES))


def _mix(proj, gates_col, gates_row, conv_w, gate_b_col, gate_b_row, norm_g, tables, batch, seq):
    L = MLSTM_CHUNK
    nc = seq // L
    t = batch * seq
    blk = CONV_DIM
    steps = batch * nc
    tab_specs = [pl.BlockSpec((tab.shape[0] // steps, tab.shape[1]), lambda b, c: (b * nc + c, 0))
                 for tab in tables]

    def slab(k):
        return pl.BlockSpec((L, blk), lambda b, c, k=k: (b * nc + c, k))

    return pl.pallas_call(
        functools.partial(_mix_kernel, n_tab=len(tables)),
        grid=(batch, nc),
        in_specs=[slab(0), slab(1), slab(2), slab(3), slab(4), slab(5), slab(6),
                  pl.BlockSpec((L, LANES), lambda b, c: (b * nc + c, 0)),
                  pl.BlockSpec((N_GATES, L), lambda b, c: (0, b * nc + c)),
                  pl.BlockSpec((3, CONV_DIM), lambda b, c: (0, 0)),
                  pl.BlockSpec((1, LANES), lambda b, c: (0, 0)),
                  pl.BlockSpec((N_GATES, 1), lambda b, c: (0, 0)),
                  pl.BlockSpec((1, MLSTM_DIM), lambda b, c: (0, 0)),
                  *tab_specs],
        out_specs=[pl.BlockSpec((L, CONV_DIM + MLSTM_DIM), lambda b, c: (b * nc + c, 0)),
                   *tab_specs],
        out_shape=[jax.ShapeDtypeStruct((t, CONV_DIM + MLSTM_DIM), BF16),
                   *[jax.ShapeDtypeStruct(tab.shape, BF16) for tab in tables]],
        scratch_shapes=[pltpu.VMEM((MLSTM_HEADS, MLSTM_HEAD_DIM, MLSTM_HEAD_DIM + LANES), F32),
                        pltpu.VMEM((8, LANES), F32),
                        pltpu.VMEM((8, CONV_DIM), F32)],
        compiler_params=_params(("parallel", "arbitrary")),
        name="mix",
    )(proj, proj, proj, proj, proj, proj, proj, gates_col, gates_row, conv_w,
      gate_b_col, gate_b_row, norm_g, *tables)


def _matmul_res_kernel(a_ref, w_ref, r_ref, o_ref):
    o_ref[...] = r_ref[...] + _dot(a_ref[...], w_ref[...])


def _matmul_res(a, w, res, tm=512):
    t, k = a.shape
    n = w.shape[1]
    return pl.pallas_call(
        _matmul_res_kernel,
        grid=(t // tm,),
        in_specs=[pl.BlockSpec((tm, k), lambda i: (i, 0)),
                  pl.BlockSpec((k, n), lambda i: (0, 0)),
                  pl.BlockSpec((tm, n), lambda i: (i, 0))],
        out_specs=pl.BlockSpec((tm, n), lambda i: (i, 0)),
        out_shape=jax.ShapeDtypeStruct((t, n), F32),
        compiler_params=_params(("parallel",)),
        name="matmul_res",
    )(a, w, res)


def _topk_rank(s, k):
    n = s.shape[0]
    iota = lax.broadcasted_iota(jnp.int32, s.shape, 0).astype(F32)
    rank = jnp.full(s.shape, float(k), F32)
    vals = []
    for r in range(k):
        mx = jnp.max(s, axis=0, keepdims=True)
        idx = jnp.min(jnp.where(s == mx, iota, float(n)), axis=0, keepdims=True)
        sel = iota == idx
        rank = jnp.where(sel, float(r), rank)
        s = jnp.where(sel, -jnp.inf, s)
        vals.append(mx)
    return rank, vals


def _route_exact(s1, s2):
    K = PEER_TOPK
    rank1, a_vals = _topk_rank(s1, K)
    rank2, b_vals = _topk_rank(s2, K)
    bmat = jnp.concatenate(b_vals, axis=0)
    cmat = jnp.concatenate([a_vals[x] + bmat for x in range(K)], axis=0)
    rank_c, c_vals = _topk_rank(cmat, K)
    chosen = (rank_c < float(K)).astype(F32)
    z = jnp.zeros_like(c_vals[0])
    for v in c_vals:
        z = z + jnp.exp(v - c_vals[0])
    cnt_i = jnp.zeros_like(s1)
    for x in range(K):
        cnt_x = jnp.sum(chosen[x * K:(x + 1) * K, :], axis=0, keepdims=True)
        cnt_i = cnt_i + jnp.where(rank1 == float(x), cnt_x, 0.0)
    return rank2, cnt_i, jnp.exp(s1 - a_vals[0]) / z, jnp.exp(s2 - b_vals[0])


def _sort16_network():
    def merge(lo, hi, r):
        step = r * 2
        if step < hi - lo:
            yield from merge(lo, hi, step)
            yield from merge(lo + r, hi, step)
            yield from [(i, i + r) for i in range(lo + r, hi - r, step)]
        else:
            yield (lo, lo + r)

    def sort(lo, hi):
        if hi - lo >= 1:
            mid = lo + (hi - lo) // 2
            yield from sort(lo, mid)
            yield from sort(mid + 1, hi)
            yield from merge(lo, hi, 1)

    return tuple(sort(0, 15))


_SORT16 = _sort16_network()
SUBLANES = 8


def _top16_sorted(tiles):
    v = list(tiles) + [None] * (16 - len(tiles))
    for i, j in _SORT16:
        a, b = v[i], v[j]
        if b is None:
            continue
        if a is None:
            v[i], v[j] = b, None
        else:
            v[i], v[j] = jnp.maximum(a, b), jnp.minimum(a, b)
    neg = jnp.full(tiles[0].shape, -jnp.inf, F32)
    v = [neg if x is None else x for x in v]
    for shift in (4, 2, 1):
        r = [pltpu.roll(x, shift, 0) for x in v]
        v = [jnp.maximum(v[k], r[15 - k]) for k in range(16)]
        d = 8
        while d >= 1:
            for k in range(16):
                if (k & d) == 0:
                    v[k], v[k + d] = jnp.maximum(v[k], v[k + d]), jnp.minimum(v[k], v[k + d])
            d //= 2
    return v


def _sorted_count(b, pred):
    c8 = pred(b[7])
    c4 = pred(jnp.where(c8, b[11], b[3]))
    c2 = pred(jnp.where(c8, jnp.where(c4, b[13], b[9]), jnp.where(c4, b[5], b[1])))
    hi = jnp.where(c4, jnp.where(c2, b[14], b[12]), jnp.where(c2, b[10], b[8]))
    lo = jnp.where(c4, jnp.where(c2, b[6], b[4]), jnp.where(c2, b[2], b[0]))
    c1 = pred(jnp.where(c8, hi, lo))
    n = (jnp.where(c8, 8.0, 0.0) + jnp.where(c4, 4.0, 0.0)) + (jnp.where(c2, 2.0, 0.0) + jnp.where(c1, 1.0, 0.0))
    return n + jnp.where(pred(b[15]), 1.0, 0.0)


def _route_sorted(s1, s2):
    K = PEER_TOPK
    nk, n = s1.shape
    nt = nk // SUBLANES
    s1t = s1.reshape(nt, SUBLANES, n)
    s2t = s2.reshape(nt, SUBLANES, n)
    a = _top16_sorted([s1t[k] for k in range(nt)])
    b = _top16_sorted([s2t[k] for k in range(nt)])
    sub = lax.broadcasted_iota(jnp.int32, (SUBLANES, n), 0)

    def pack(vals):
        out = vals[0]
        for r in range(1, SUBLANES):
            out = jnp.where(sub == r, vals[r], out)
        return out

    a_lo, a_hi, b_hi = pack(a[:8]), pack(a[8:]), pack(b[8:])
    cands = [a_lo + b[0], a_hi + b[0]]
    for y in range(1, 8):
        cands.append(jnp.where(sub <= K // (y + 1) - 1, a_lo + b[y], -jnp.inf))
    cands.append(a[0] + b_hi)
    cs = _top16_sorted(cands)
    tau = cs[K - 1]
    z = jnp.zeros_like(tau)
    for c in cs:
        z = z + jnp.exp(c - cs[0])
    n3 = jnp.zeros_like(tau)
    for c in cands:
        n3 = n3 + jnp.where(c >= tau, 1.0, 0.0)
    n3 = jnp.sum(n3, axis=0, keepdims=True)

    in1 = s1t >= a[K - 1]
    in2 = s2t >= b[K - 1]
    n1 = jnp.sum(jnp.sum(jnp.where(in1, 1.0, 0.0), axis=0), axis=0, keepdims=True)
    n2 = jnp.sum(jnp.sum(jnp.where(in2, 1.0, 0.0), axis=0), axis=0, keepdims=True)
    bad = jnp.where((n1 != float(K)) | (n2 != float(K)) | (n3 != float(K)), 1.0, 0.0)

    cnt_i = _sorted_count(b, lambda p: s1t + p >= tau)
    r2 = _sorted_count(b, lambda p: p > s2t)
    cnt_i = jnp.where(in1, cnt_i, 0.0)
    ai = jnp.exp(s1t - a[0]) * (1.0 / z)
    bj = jnp.exp(s2t - b[0])
    return (r2.reshape(nk, n), cnt_i.reshape(nk, n), ai.reshape(nk, n), bj.reshape(nk, n), bad)


def _peer_select_kernel(h_ref, g_ref, wq_ref, keys_ref, tab_ref, xn_ref, r2_ref, bj_ref, cnt_ref,
                        ai_ref, tab_out_ref, q_ref):
    hidx = pl.program_id(1)
    tab_out_ref[...] = tab_ref[...].astype(BF16)

    @pl.when(hidx == 0)
    def _():
        xn = _rms(h_ref[...], g_ref[...]).astype(BF16)
        xn_ref[...] = xn
        q = _dot(xn, wq_ref[...]).astype(BF16)
        for g in range(2 * PEER_HEADS):
            q_ref[g] = q[:, g * PEER_HALF:(g + 1) * PEER_HALF]

    s1 = _dot_nt(keys_ref[0, 0], q_ref[2 * hidx])
    s2 = _dot_nt(keys_ref[0, 1], q_ref[2 * hidx + 1])

    def emit(r2, cnt_i, ai, bj):
        r2_ref[0] = r2.astype(BF16)
        bj_ref[0] = bj.astype(BF16)
        for k in range(cnt_i.shape[0] // SUBLANES):
            rs = slice(k * SUBLANES, (k + 1) * SUBLANES)
            cnt_ref[pl.ds(k * SUBLANES, SUBLANES), hidx, :] = cnt_i[rs]
            ai_ref[pl.ds(k * SUBLANES, SUBLANES), hidx, :] = ai[rs]

    r2, cnt_i, ai, bj, bad = _route_sorted(s1, s2)
    emit(r2, cnt_i, ai, bj)

    @pl.when(jnp.max(bad) > 0.0)
    def _():
        emit(*_route_exact(s1, s2))


def _peer_select(h1, g, wq, keys, table, tt=256):
    t, d = h1.shape
    nk = PEER_N_KEYS
    steps = (t // tt) * PEER_HEADS
    tab_spec = pl.BlockSpec((table.shape[0] // steps, table.shape[1]),
                            lambda i, h: (i * PEER_HEADS + h, 0))
    stat_spec = pl.BlockSpec((nk, PEER_HEADS, tt), lambda i, h: (0, 0, i))
    stat16 = jax.ShapeDtypeStruct((PEER_HEADS, nk, t), BF16)
    stat16_spec = pl.BlockSpec((1, nk, tt), lambda i, h: (h, 0, i))
    return pl.pallas_call(
        _peer_select_kernel,
        grid=(t // tt, PEER_HEADS),
        in_specs=[pl.BlockSpec((tt, d), lambda i, h: (i, 0)),
                  pl.BlockSpec((1, d), lambda i, h: (0, 0)),
                  pl.BlockSpec((d, wq.shape[1]), lambda i, h: (0, 0)),
                  pl.BlockSpec((1, 2, nk, PEER_HALF), lambda i, h: (h, 0, 0, 0)),
                  tab_spec],
        out_specs=[pl.BlockSpec((tt, d), lambda i, h: (i, 0)),
                   stat16_spec, stat16_spec, stat_spec, stat_spec, tab_spec],
        out_shape=[jax.ShapeDtypeStruct((t, d), BF16), stat16, stat16,
                   jax.ShapeDtypeStruct((nk, PEER_HEADS, t), F32),
                   jax.ShapeDtypeStruct((nk, PEER_HEADS, t), F32),
                   jax.ShapeDtypeStruct(table.shape, BF16)],
        scratch_shapes=[pltpu.VMEM((2 * PEER_HEADS, tt, PEER_HALF), BF16)],
        compiler_params=_params(("parallel", "arbitrary")),
        name="peer_select",
    )(h1, g, wq, keys, table)


def _gelu(a):
    return 0.5 * a * (1.0 + lax.erf(a * (1.0 / math.sqrt(2.0))))


def _peer_dense_kernel(xn_ref, u_ref, v_ref, r2_ref, bj_ref, cnt_ref, ai_ref, o_ref, m_ref):
    @pl.when(pl.program_id(1) == 0)
    def _():
        o_ref[...] = jnp.zeros_like(o_ref)

    tt = xn_ref.shape[0]
    et = u_ref.shape[0]
    nk = PEER_N_KEYS
    rows = 2 * SUBLANES
    tl = 2 * LANES
    zero = jnp.zeros((rows, tl), BF16)
    for il in range(et // nk):
        for tb in range(tt // tl):
            ls = slice(tb * tl, (tb + 1) * tl)
            acc = [None] * (nk // rows)
            for h in range(PEER_HEADS):
                c16 = jnp.broadcast_to(cnt_ref[il, h:h + 1, ls], (rows, tl)).astype(BF16)
                a16 = jnp.broadcast_to(ai_ref[il, h:h + 1, ls], (rows, tl)).astype(BF16)
                for jg in range(nk // rows):
                    js = slice(jg * rows, (jg + 1) * rows)
                    term = jnp.where(r2_ref[h, js, ls] < c16, a16 * bj_ref[h, js, ls], zero)
                    acc[jg] = term if acc[jg] is None else acc[jg] + term
            blk = jnp.concatenate(acc, axis=0).astype(F32)
            m_ref[ls, il * nk:(il + 1) * nk] = blk.T.astype(BF16)
    a = _dot_nt(xn_ref[...], u_ref[...])
    w = _gelu(a).astype(BF16) * m_ref[...]
    o_ref[...] += _dot(w, v_ref[...])


def _peer_dense(xn, u, v, r2, bj, cnt, ai, tt=1024, et=512):
    t, d = xn.shape
    ne = u.shape[0]
    nk = PEER_N_KEYS
    per_j = pl.BlockSpec((PEER_HEADS, nk, tt), lambda i, e: (0, 0, i))
    per_i = pl.BlockSpec((et // nk, PEER_HEADS, tt), lambda i, e: (e, 0, i))
    return pl.pallas_call(
        _peer_dense_kernel,
        grid=(t // tt, ne // et),
        in_specs=[pl.BlockSpec((tt, d), lambda i, e: (i, 0)),
                  pl.BlockSpec((et, d), lambda i, e: (e, 0)),
                  pl.BlockSpec((et, d), lambda i, e: (e, 0)),
                  per_j, per_j, per_i, per_i],
        out_specs=pl.BlockSpec((tt, d), lambda i, e: (i, 0)),
        out_shape=jax.ShapeDtypeStruct((t, d), F32),
        scratch_shapes=[pltpu.VMEM((tt, et), BF16)],
        compiler_params=_params(("parallel", "arbitrary")),
        name="peer_dense",
    )(xn, u, v, r2, bj, cnt, ai)


def _final_kernel(h1_ref, peer_ref, p_ref, g3_ref, wg_ref, wp_ref, gf_ref, o_ref, *, last):
    h2 = h1_ref[...] + peer_ref[...]
    xn = _rms(h2, g3_ref[...]).astype(BF16)
    gate = jax.nn.sigmoid(_dot(xn, wg_ref[...]))
    h3 = h2 + gate * _dot(p_ref[...].astype(BF16), wp_ref[...])
    o_ref[...] = _rms(h3, gf_ref[...]) if last else h3


def _final(h1, peer, p2, g3, wg, wp, gf, last, tm=256):
    t, d = h1.shape
    pd = p2.shape[1]
    row = pl.BlockSpec((tm, d), lambda i: (i, 0))
    vec = pl.BlockSpec((1, d), lambda i: (0, 0))
    return pl.pallas_call(
        functools.partial(_final_kernel, last=last),
        grid=(t // tm,),
        in_specs=[row, row, pl.BlockSpec((tm, pd), lambda i: (i, 0)), vec,
                  pl.BlockSpec((d, d), lambda i: (0, 0)),
                  pl.BlockSpec((pd, d), lambda i: (0, 0)), vec],
        out_specs=row,
        out_shape=jax.ShapeDtypeStruct((t, d), F32),
        compiler_params=_params(("parallel",)),
        name="ple_final",
    )(h1, peer, p2, g3, wg, wp, gf)


def kernel(x, p, norm_mix_g, w_in, conv_w, mlstm_gate_b, mlstm_norm_g, w_out, norm_ffn_g,
           peer_wq, peer_keys, peer_u, peer_v, norm_ple_g, ple_w_gate, ple_w_proj, norm_final_g):
    bsz, seq, d = x.shape
    depth = w_in.shape[0]
    t = bsz * seq
    h = x.reshape(t, d)
    for i in range(depth):
        w_all = w_in[i].astype(BF16)
        w_gates = jnp.pad(w_all[:, MAIN_COLS:], ((0, 0), (0, LANES - N_GATES)))
        proj, gates = _rms_proj(h, norm_mix_g[i].reshape(1, d), w_all, w_gates)
        gate_b = mlstm_gate_b[i].astype(F32)
        mix, u_bf, wo_bf, wq_bf, wg_bf = _mix(
            proj, gates, gates[:, :N_GATES].T, conv_w[i],
            jnp.pad(gate_b, (0, LANES - N_GATES)).reshape(1, LANES),
            gate_b.reshape(N_GATES, 1), mlstm_norm_g[i].reshape(1, MLSTM_DIM),
            (peer_u[i], w_out[i], peer_wq[i], ple_w_gate[i]), bsz, seq)
        h1 = _matmul_res(mix, wo_bf, h)
        xn, r2, bj, cnt, ai, v_bf = _peer_select(h1, norm_ffn_g[i].reshape(1, d), wq_bf,
                                                 peer_keys[i].astype(BF16), peer_v[i])
        peer = _peer_dense(xn, u_bf, v_bf, r2, bj, cnt, ai)
        h = _final(h1, peer, p[i].reshape(t, -1), norm_ple_g[i].reshape(1, d),
                   wg_bf, ple_w_proj[i].astype(BF16),
                   norm_final_g.reshape(1, d), i == depth - 1)
    return h.reshape(bsz, seq, d)
```

```python
import functools
import math

import jax
import jax.numpy as jnp
from jax import lax
from jax.experimental import pallas as pl
from jax.experimental.pallas import tpu as pltpu

F32 = jnp.float32
BF16 = jnp.bfloat16

D_MODEL = 2048
CONV_DIM = 1024
MLSTM_HEADS = 4
MLSTM_HEAD_DIM = 256
MLSTM_DIM = MLSTM_HEADS * MLSTM_HEAD_DIM
MLSTM_CHUNK = 128
MAIN_COLS = 3 * CONV_DIM + 4 * MLSTM_DIM
N_GATES = 2 * MLSTM_HEADS
PEER_HEADS = 8
PEER_N_KEYS = 128
PEER_HALF = 128
PEER_TOPK = 16
EPS = 1e-6

LANES = 128
VMEM_LIMIT = 60 * 1024 * 1024


def _params(semantics):
    return pltpu.CompilerParams(dimension_semantics=semantics, vmem_limit_bytes=VMEM_LIMIT)


def _rms(x, g):
    r = lax.rsqrt(jnp.mean(x * x, axis=-1, keepdims=True) + EPS)
    return (x * r) * g


def _dot(a, b):
    return jnp.dot(a, b, preferred_element_type=F32)


def _dot_nt(a, b):
    return lax.dot_general(a, b, (((1,), (1,)), ((), ())), preferred_element_type=F32)


def _rms_proj_kernel(x_ref, g_ref, w_ref, wg_ref, o_ref, og_ref, xn_ref):
    @pl.when(pl.program_id(1) == 0)
    def _():
        xn = _rms(x_ref[...], g_ref[...]).astype(BF16)
        xn_ref[...] = xn
        og_ref[...] = _dot(xn, wg_ref[...])

    o_ref[...] = _dot(xn_ref[...], w_ref[...])


def _rms_proj(x2, g, w_all, w_gate, tm=1024, tn=1024):
    t, d = x2.shape
    n = MAIN_COLS
    return pl.pallas_call(
        _rms_proj_kernel,
        grid=(t // tm, n // tn),
        in_specs=[
            pl.BlockSpec((tm, d), lambda i, j: (i, 0)),
            pl.BlockSpec((1, d), lambda i, j: (0, 0)),
            pl.BlockSpec((d, tn), lambda i, j: (0, j)),
            pl.BlockSpec((d, LANES), lambda i, j: (0, 0)),
        ],
        out_specs=[
            pl.BlockSpec((tm, tn), lambda i, j: (i, j)),
            pl.BlockSpec((tm, LANES), lambda i, j: (i, 0)),
        ],
        out_shape=[
            jax.ShapeDtypeStruct((t, n), F32),
            jax.ShapeDtypeStruct((t, LANES), F32),
        ],
        scratch_shapes=[pltpu.VMEM((tm, d), BF16)],
        compiler_params=_params(("parallel", "arbitrary")),
        name="rms_proj",
    )(x2, g, w_all, w_gate)


def _log_sigmoid(x):
    return jnp.minimum(x, 0.0) - jnp.log1p(jnp.exp(-jnp.abs(x)))


def _mix_kernel(bc_ref, cc_ref, uc_ref, q_ref, k_ref, v_ref, o_ref, gc_ref, gr_ref,
                convw_ref, gbc_ref, gbr_ref, ng_ref, *rest, n_tab):
    tab_refs = rest[:n_tab]
    y_ref = rest[n_tab]
    tab_out_refs = rest[n_tab + 1:2 * n_tab + 1]
    caug_ref, m_ref, zc_ref = rest[2 * n_tab + 1:]
    L = MLSTM_CHUNK
    hd = MLSTM_HEAD_DIM

    for src, dst in zip(tab_refs, tab_out_refs):
        dst[...] = src[...].astype(BF16)

    @pl.when(pl.program_id(1) == 0)
    def _():
        caug_ref[...] = jnp.zeros_like(caug_ref)
        m_ref[...] = jnp.zeros_like(m_ref)
        zc_ref[...] = jnp.zeros_like(zc_ref)

    z = cc_ref[...] * uc_ref[...]
    row = lax.broadcasted_iota(jnp.int32, z.shape, 0)
    prev = zc_ref[...]
    p1 = prev[7:8, :]
    p2 = prev[6:7, :]
    z1 = jnp.where(row == 0, p1, pltpu.roll(z, 1, 0))
    z2 = jnp.where(row == 0, p2, jnp.where(row == 1, p1, pltpu.roll(z, 2, 0)))
    cw = convw_ref[...]
    y_conv = bc_ref[...] * (cw[0:1, :] * z2 + cw[1:2, :] * z1 + cw[2:3, :] * z)
    zc_ref[...] = z[L - 8:, :]
    y_ref[:, :CONV_DIM] = y_conv.astype(y_ref.dtype)

    gcol = gc_ref[...] + gbc_ref[...]
    grow = gr_ref[...] + gbr_ref[...]
    ri = lax.broadcasted_iota(jnp.int32, (L, L), 0)
    ci = lax.broadcasted_iota(jnp.int32, (L, L), 1)
    causal = ri >= ci
    tril = causal.astype(F32)
    triu = (ri <= ci).astype(F32)
    hi = lax.Precision.HIGHEST
    b_col_all = jnp.dot(tril, _log_sigmoid(gcol), precision=hi, preferred_element_type=F32)
    b_row_all = jnp.dot(_log_sigmoid(grow), triu, precision=hi, preferred_element_type=F32)
    lane = lax.broadcasted_iota(jnp.int32, (L, LANES), 1)
    ones_col = (lane == 0).astype(BF16)
    scale = hd ** -0.5

    for h in range(MLSTM_HEADS):
        cs = slice(h * hd, (h + 1) * hd)
        qh = q_ref[:, cs].astype(BF16)
        kh = k_ref[:, cs]
        vaug = jnp.concatenate([v_ref[:, cs].astype(BF16), ones_col], axis=1)
        i_row = grow[h:h + 1, :]
        i_col = gcol[:, h:h + 1]
        b_row = b_row_all[MLSTM_HEADS + h:MLSTM_HEADS + h + 1, :]
        b_col = b_col_all[:, MLSTM_HEADS + h:MLSTM_HEADS + h + 1]
        m_prev = m_ref[h:h + 1, 0:1]

        dmat = jnp.where(causal, b_col - b_row + i_row, -jnp.inf)
        inter = b_col + m_prev
        m_t = jnp.maximum(inter, jnp.max(dmat, axis=-1, keepdims=True))
        pw = jnp.exp(dmat - m_t)
        sc = (_dot_nt(qh, kh.astype(BF16)) * scale) * pw
        a = jnp.exp(inter - m_t)
        caug = caug_ref[h]
        nd = a * _dot(qh, caug.astype(BF16)) + _dot(sc.astype(BF16), vaug)
        num = nd[:, :hd]
        den = nd[:, hd:hd + 1]
        hh = num / jnp.maximum(jnp.abs(den), jnp.exp(-m_t))
        hn = _rms(hh, ng_ref[:, cs])
        y_ref[:, CONV_DIM + h * hd:CONV_DIM + (h + 1) * hd] = (
            jax.nn.sigmoid(o_ref[:, cs]) * hn).astype(y_ref.dtype)

        m_new = m_t[L - 1:L, :]
        b_last = b_col[L - 1:L, :]
        w_s = jnp.exp(b_last - b_col + i_col - m_new)
        a_l = jnp.exp(b_last + m_prev - m_new)
        kw = (kh * (w_s * scale)).astype(BF16)
        caug_ref[h] = a_l * caug + _dot(kw.T, vaug)
        m_ref[h:h + 1, :] = jnp.broadcast_to(m_new, (1, LANES))


def _mix(proj, gates_col, gates_row, conv_w, gate_b_col, gate_b_row, norm_g, tables, batch, seq):
    L = MLSTM_CHUNK
    nc = seq // L
    t = batch * seq
    blk = CONV_DIM
    steps = batch * nc
    tab_specs = [pl.BlockSpec((tab.shape[0] // steps, tab.shape[1]), lambda b, c: (b * nc + c, 0))
                 for tab in tables]

    def slab(k):
        return pl.BlockSpec((L, blk), lambda b, c, k=k: (b * nc + c, k))

    return pl.pallas_call(
        functools.partial(_mix_kernel, n_tab=len(tables)),
        grid=(batch, nc),
        in_specs=[slab(0), slab(1), slab(2), slab(3), slab(4), slab(5), slab(6),
                  pl.BlockSpec((L, LANES), lambda b, c: (b * nc + c, 0)),
                  pl.BlockSpec((N_GATES, L), lambda b, c: (0, b * nc + c)),
                  pl.BlockSpec((3, CONV_DIM), lambda b, c: (0, 0)),
                  pl.BlockSpec((1, LANES), lambda b, c: (0, 0)),
                  pl.BlockSpec((N_GATES, 1), lambda b, c: (0, 0)),
                  pl.BlockSpec((1, MLSTM_DIM), lambda b, c: (0, 0)),
                  *tab_specs],
        out_specs=[pl.BlockSpec((L, CONV_DIM + MLSTM_DIM), lambda b, c: (b * nc + c, 0)),
                   *tab_specs],
        out_shape=[jax.ShapeDtypeStruct((t, CONV_DIM + MLSTM_DIM), BF16),
                   *[jax.ShapeDtypeStruct(tab.shape, BF16) for tab in tables]],
        scratch_shapes=[pltpu.VMEM((MLSTM_HEADS, MLSTM_HEAD_DIM, MLSTM_HEAD_DIM + LANES), F32),
                        pltpu.VMEM((8, LANES), F32),
                        pltpu.VMEM((8, CONV_DIM), F32)],
        compiler_params=_params(("parallel", "arbitrary")),
        name="mix",
    )(proj, proj, proj, proj, proj, proj, proj, gates_col, gates_row, conv_w,
      gate_b_col, gate_b_row, norm_g, *tables)


def _matmul_res_kernel(a_ref, w_ref, r_ref, o_ref):
    o_ref[...] = r_ref[...] + _dot(a_ref[...], w_ref[...])


def _matmul_res(a, w, res, tm=512):
    t, k = a.shape
    n = w.shape[1]
    return pl.pallas_call(
        _matmul_res_kernel,
        grid=(t // tm,),
        in_specs=[pl.BlockSpec((tm, k), lambda i: (i, 0)),
                  pl.BlockSpec((k, n), lambda i: (0, 0)),
                  pl.BlockSpec((tm, n), lambda i: (i, 0))],
        out_specs=pl.BlockSpec((tm, n), lambda i: (i, 0)),
        out_shape=jax.ShapeDtypeStruct((t, n), F32),
        compiler_params=_params(("parallel",)),
        name="matmul_res",
    )(a, w, res)


def _topk_rank(s, k):
    n = s.shape[0]
    iota = lax.broadcasted_iota(jnp.int32, s.shape, 0).astype(F32)
    rank = jnp.full(s.shape, float(k), F32)
    vals = []
    for r in range(k):
        mx = jnp.max(s, axis=0, keepdims=True)
        idx = jnp.min(jnp.where(s == mx, iota, float(n)), axis=0, keepdims=True)
        sel = iota == idx
        rank = jnp.where(sel, float(r), rank)
        s = jnp.where(sel, -jnp.inf, s)
        vals.append(mx)
    return rank, vals


def _route_exact(s1, s2):
    K = PEER_TOPK
    rank1, a_vals = _topk_rank(s1, K)
    rank2, b_vals = _topk_rank(s2, K)
    bmat = jnp.concatenate(b_vals, axis=0)
    cmat = jnp.concatenate([a_vals[x] + bmat for x in range(K)], axis=0)
    rank_c, c_vals = _topk_rank(cmat, K)
    chosen = (rank_c < float(K)).astype(F32)
    z = jnp.zeros_like(c_vals[0])
    for v in c_vals:
        z = z + jnp.exp(v - c_vals[0])
    cnt_i = jnp.zeros_like(s1)
    for x in range(K):
        cnt_x = jnp.sum(chosen[x * K:(x + 1) * K, :], axis=0, keepdims=True)
        cnt_i = cnt_i + jnp.where(rank1 == float(x), cnt_x, 0.0)
    return rank2, cnt_i, jnp.exp(s1 - a_vals[0]) / z, jnp.exp(s2 - b_vals[0])


def _sort16_network():
    def merge(lo, hi, r):
        step = r * 2
        if step < hi - lo:
            yield from merge(lo, hi, step)
            yield from merge(lo + r, hi, step)
            yield from [(i, i + r) for i in range(lo + r, hi - r, step)]
        else:
            yield (lo, lo + r)

    def sort(lo, hi):
        if hi - lo >= 1:
            mid = lo + (hi - lo) // 2
            yield from sort(lo, mid)
            yield from sort(mid + 1, hi)
            yield from merge(lo, hi, 1)

    return tuple(sort(0, 15))


_SORT16 = _sort16_network()
SUBLANES = 8


def _top16_sorted(tiles):
    v = list(tiles) + [None] * (16 - len(tiles))
    for i, j in _SORT16:
        a, b = v[i], v[j]
        if b is None:
            continue
        if a is None:
            v[i], v[j] = b, None
        else:
            v[i], v[j] = jnp.maximum(a, b), jnp.minimum(a, b)
    neg = jnp.full(tiles[0].shape, -jnp.inf, F32)
    v = [neg if x is None else x for x in v]
    for shift in (4, 2, 1):
        r = [pltpu.roll(x, shift, 0) for x in v]
        v = [jnp.maximum(v[k], r[15 - k]) for k in range(16)]
        d = 8
        while d >= 1:
            for k in range(16):
                if (k & d) == 0:
                    v[k], v[k + d] = jnp.maximum(v[k], v[k + d]), jnp.minimum(v[k], v[k + d])
            d //= 2
    return v


def _sorted_count(b, pred):
    c8 = pred(b[7])
    c4 = pred(jnp.where(c8, b[11], b[3]))
    c2 = pred(jnp.where(c8, jnp.where(c4, b[13], b[9]), jnp.where(c4, b[5], b[1])))
    hi = jnp.where(c4, jnp.where(c2, b[14], b[12]), jnp.where(c2, b[10], b[8]))
    lo = jnp.where(c4, jnp.where(c2, b[6], b[4]), jnp.where(c2, b[2], b[0]))
    c1 = pred(jnp.where(c8, hi, lo))
    n = (jnp.where(c8, 8.0, 0.0) + jnp.where(c4, 4.0, 0.0)) + (jnp.where(c2, 2.0, 0.0) + jnp.where(c1, 1.0, 0.0))
    return n + jnp.where(pred(b[15]), 1.0, 0.0)


def _route_sorted(s1, s2):
    K = PEER_TOPK
    nk, n = s1.shape
    nt = nk // SUBLANES
    s1t = s1.reshape(nt, SUBLANES, n)
    s2t = s2.reshape(nt, SUBLANES, n)
    a = _top16_sorted([s1t[k] for k in range(nt)])
    b = _top16_sorted([s2t[k] for k in range(nt)])
    sub = lax.broadcasted_iota(jnp.int32, (SUBLANES, n), 0)

    def pack(vals):
        out = vals[0]
        for r in range(1, SUBLANES):
            out = jnp.where(sub == r, vals[r], out)
        return out

    a_lo, a_hi, b_hi = pack(a[:8]), pack(a[8:]), pack(b[8:])
    cands = [a_lo + b[0], a_hi + b[0]]
    for y in range(1, 8):
        cands.append(jnp.where(sub <= K // (y + 1) - 1, a_lo + b[y], -jnp.inf))
    cands.append(a[0] + b_hi)
    cs = _top16_sorted(cands)
    tau = cs[K - 1]
    z = jnp.zeros_like(tau)
    for c in cs:
        z = z + jnp.exp(c - cs[0])
    n3 = jnp.zeros_like(tau)
    for c in cands:
        n3 = n3 + jnp.where(c >= tau, 1.0, 0.0)
    n3 = jnp.sum(n3, axis=0, keepdims=True)

    in1 = s1t >= a[K - 1]
    in2 = s2t >= b[K - 1]
    n1 = jnp.sum(jnp.sum(jnp.where(in1, 1.0, 0.0), axis=0), axis=0, keepdims=True)
    n2 = jnp.sum(jnp.sum(jnp.where(in2, 1.0, 0.0), axis=0), axis=0, keepdims=True)
    bad = jnp.where((n1 != float(K)) | (n2 != float(K)) | (n3 != float(K)), 1.0, 0.0)

    cnt_i = _sorted_count(b, lambda p: s1t + p >= tau)
    r2 = _sorted_count(b, lambda p: p > s2t)
    cnt_i = jnp.where(in1, cnt_i, 0.0)
    ai = jnp.exp(s1t - a[0]) * (1.0 / z)
    bj = jnp.exp(s2t - b[0])
    return (r2.reshape(nk, n), cnt_i.reshape(nk, n), ai.reshape(nk, n), bj.reshape(nk, n), bad)


def _peer_select_kernel(h_ref, g_ref, wq_ref, keys_ref, tab_ref, xn_ref, r2_ref, bj_ref, cnt_ref,
                        ai_ref, tab_out_ref, q_ref):
    hidx = pl.program_id(1)
    tab_out_ref[...] = tab_ref[...].astype(BF16)

    @pl.when(hidx == 0)
    def _():
        xn = _rms(h_ref[...], g_ref[...]).astype(BF16)
        xn_ref[...] = xn
        q = _dot(xn, wq_ref[...]).astype(BF16)
        for g in range(2 * PEER_HEADS):
            q_ref[g] = q[:, g * PEER_HALF:(g + 1) * PEER_HALF]

    s1 = _dot_nt(keys_ref[0, 0], q_ref[2 * hidx])
    s2 = _dot_nt(keys_ref[0, 1], q_ref[2 * hidx + 1])

    def emit(r2, cnt_i, ai, bj):
        r2_ref[0] = r2.astype(BF16)
        bj_ref[0] = bj.astype(BF16)
        for k in range(cnt_i.shape[0] // SUBLANES):
            rs = slice(k * SUBLANES, (k + 1) * SUBLANES)
            cnt_ref[pl.ds(k * SUBLANES, SUBLANES), hidx, :] = cnt_i[rs]
            ai_ref[pl.ds(k * SUBLANES, SUBLANES), hidx, :] = ai[rs]

    r2, cnt_i, ai, bj, bad = _route_sorted(s1, s2)
    emit(r2, cnt_i, ai, bj)

    @pl.when(jnp.max(bad) > 0.0)
    def _():
        emit(*_route_exact(s1, s2))


def _peer_select(h1, g, wq, keys, table, tt=512):
    t, d = h1.shape
    nk = PEER_N_KEYS
    steps = (t // tt) * PEER_HEADS
    tab_spec = pl.BlockSpec((table.shape[0] // steps, table.shape[1]),
                            lambda i, h: (i * PEER_HEADS + h, 0))
    stat_spec = pl.BlockSpec((nk, PEER_HEADS, tt), lambda i, h: (0, 0, i))
    stat16 = jax.ShapeDtypeStruct((PEER_HEADS, nk, t), BF16)
    stat16_spec = pl.BlockSpec((1, nk, tt), lambda i, h: (h, 0, i))
    return pl.pallas_call(
        _peer_select_kernel,
        grid=(t // tt, PEER_HEADS),
        in_specs=[pl.BlockSpec((tt, d), lambda i, h: (i, 0)),
                  pl.BlockSpec((1, d), lambda i, h: (0, 0)),
                  pl.BlockSpec((d, wq.shape[1]), lambda i, h: (0, 0)),
                  pl.BlockSpec((1, 2, nk, PEER_HALF), lambda i, h: (h, 0, 0, 0)),
                  tab_spec],
        out_specs=[pl.BlockSpec((tt, d), lambda i, h: (i, 0)),
                   stat16_spec, stat16_spec, stat_spec, stat_spec, tab_spec],
        out_shape=[jax.ShapeDtypeStruct((t, d), BF16), stat16, stat16,
                   jax.ShapeDtypeStruct((nk, PEER_HEADS, t), F32),
                   jax.ShapeDtypeStruct((nk, PEER_HEADS, t), F32),
                   jax.ShapeDtypeStruct(table.shape, BF16)],
        scratch_shapes=[pltpu.VMEM((2 * PEER_HEADS, tt, PEER_HALF), BF16)],
        compiler_params=_params(("parallel", "arbitrary")),
        name="peer_select",
    )(h1, g, wq, keys, table)


def _gelu(a):
    return 0.5 * a * (1.0 + lax.erf(a * (1.0 / math.sqrt(2.0))))


def _peer_dense_kernel(xn_ref, u_ref, v_ref, r2_ref, bj_ref, cnt_ref, ai_ref, o_ref, m_ref):
    @pl.when(pl.program_id(1) == 0)
    def _():
        o_ref[...] = jnp.zeros_like(o_ref)

    tt = xn_ref.shape[0]
    et = u_ref.shape[0]
    nk = PEER_N_KEYS
    rows = 2 * SUBLANES
    tl = 2 * LANES
    zero = jnp.zeros((rows, tl), BF16)
    for il in range(et // nk):
        for tb in range(tt // tl):
            ls = slice(tb * tl, (tb + 1) * tl)
            acc = [None] * (nk // rows)
            for h in range(PEER_HEADS):
                c16 = jnp.broadcast_to(cnt_ref[il, h:h + 1, ls], (rows, tl)).astype(BF16)
                a16 = jnp.broadcast_to(ai_ref[il, h:h + 1, ls], (rows, tl)).astype(BF16)
                for jg in range(nk // rows):
                    js = slice(jg * rows, (jg + 1) * rows)
                    term = jnp.where(r2_ref[h, js, ls] < c16, a16 * bj_ref[h, js, ls], zero)
                    acc[jg] = term if acc[jg] is None else acc[jg] + term
            blk = jnp.concatenate(acc, axis=0).astype(F32)
            m_ref[ls, il * nk:(il + 1) * nk] = blk.T.astype(BF16)
    a = _dot_nt(xn_ref[...], u_ref[...])
    w = _gelu(a).astype(BF16) * m_ref[...]
    o_ref[...] += _dot(w, v_ref[...])


def _peer_dense(xn, u, v, r2, bj, cnt, ai, tt=1024, et=512):
    t, d = xn.shape
    ne = u.shape[0]
    nk = PEER_N_KEYS
    per_j = pl.BlockSpec((PEER_HEADS, nk, tt), lambda i, e: (0, 0, i))
    per_i = pl.BlockSpec((et // nk, PEER_HEADS, tt), lambda i, e: (e, 0, i))
    return pl.pallas_call(
        _peer_dense_kernel,
        grid=(t // tt, ne // et),
        in_specs=[pl.BlockSpec((tt, d), lambda i, e: (i, 0)),
                  pl.BlockSpec((et, d), lambda i, e: (e, 0)),
                  pl.BlockSpec((et, d), lambda i, e: (e, 0)),
                  per_j, per_j, per_i, per_i],
        out_specs=pl.BlockSpec((tt, d), lambda i, e: (i, 0)),
        out_shape=jax.ShapeDtypeStruct((t, d), F32),
        scratch_shapes=[pltpu.VMEM((tt, et), BF16)],
        compiler_params=_params(("parallel", "arbitrary")),
        name="peer_dense",
    )(xn, u, v, r2, bj, cnt, ai)


def _final_kernel(h1_ref, peer_ref, p_ref, g3_ref, wg_ref, wp_ref, gf_ref, o_ref, *, last):
    h2 = h1_ref[...] + peer_ref[...]
    xn = _rms(h2, g3_ref[...]).astype(BF16)
    gate = jax.nn.sigmoid(_dot(xn, wg_ref[...]))
    h3 = h2 + gate * _dot(p_ref[...].astype(BF16), wp_ref[...])
    o_ref[...] = _rms(h3, gf_ref[...]) if last else h3


def _final(h1, peer, p2, g3, wg, wp, gf, last, tm=256):
    t, d = h1.shape
    pd = p2.shape[1]
    row = pl.BlockSpec((tm, d), lambda i: (i, 0))
    vec = pl.BlockSpec((1, d), lambda i: (0, 0))
    return pl.pallas_call(
        functools.partial(_final_kernel, last=last),
        grid=(t // tm,),
        in_specs=[row, row, pl.BlockSpec((tm, pd), lambda i: (i, 0)), vec,
                  pl.BlockSpec((d, d), lambda i: (0, 0)),
                  pl.BlockSpec((pd, d), lambda i: (0, 0)), vec],
        out_specs=row,
        out_shape=jax.ShapeDtypeStruct((t, d), F32),
        compiler_params=_params(("parallel",)),
        name="ple_final",
    )(h1, peer, p2, g3, wg, wp, gf)


def kernel(x, p, norm_mix_g, w_in, conv_w, mlstm_gate_b, mlstm_norm_g, w_out, norm_ffn_g,
           peer_wq, peer_keys, peer_u, peer_v, norm_ple_g, ple_w_gate, ple_w_proj, norm_final_g):
    bsz, seq, d = x.shape
    depth = w_in.shape[0]
    t = bsz * seq
    h = x.reshape(t, d)
    for i in range(depth):
        w_all = w_in[i].astype(BF16)
        w_gates = jnp.pad(w_all[:, MAIN_COLS:], ((0, 0), (0, LANES - N_GATES)))
        proj, gates = _rms_proj(h, norm_mix_g[i].reshape(1, d), w_all, w_gates)
        gate_b = mlstm_gate_b[i].astype(F32)
        mix, u_bf, wo_bf, wq_bf, wg_bf = _mix(
            proj, gates, gates[:, :N_GATES].T, conv_w[i],
            jnp.pad(gate_b, (0, LANES - N_GATES)).reshape(1, LANES),
            gate_b.reshape(N_GATES, 1), mlstm_norm_g[i].reshape(1, MLSTM_DIM),
            (peer_u[i], w_out[i], peer_wq[i], ple_w_gate[i]), bsz, seq)
        h1 = _matmul_res(mix, wo_bf, h)
        xn, r2, bj, cnt, ai, v_bf = _peer_select(h1, norm_ffn_g[i].reshape(1, d), wq_bf,
                                                 peer_keys[i].astype(BF16), peer_v[i])
        peer = _peer_dense(xn, u_bf, v_bf, r2, bj, cnt, ai)
        h = _final(h1, peer, p[i].reshape(t, -1), norm_ple_g[i].reshape(1, d),
                   wg_bf, ple_w_proj[i].astype(BF16),
                   norm_final_g.reshape(1, d), i == depth - 1)
    return h.reshape(bsz, seq, d)
```

```python
import functools
import math

import jax
import jax.numpy as jnp
from jax import lax
from jax.experimental import pallas as pl
from jax.experimental.pallas import tpu as pltpu

F32 = jnp.float32
BF16 = jnp.bfloat16

D_MODEL = 2048
CONV_DIM = 1024
MLSTM_HEADS = 4
MLSTM_HEAD_DIM = 256
MLSTM_DIM = MLSTM_HEADS * MLSTM_HEAD_DIM
MLSTM_CHUNK = 128
MAIN_COLS = 3 * CONV_DIM + 4 * MLSTM_DIM
N_GATES = 2 * MLSTM_HEADS
PEER_HEADS = 8
PEER_N_KEYS = 128
PEER_HALF = 128
PEER_TOPK = 16
EPS = 1e-6

LANES = 128
VMEM_LIMIT = 60 * 1024 * 1024


def _params(semantics):
    return pltpu.CompilerParams(dimension_semantics=semantics, vmem_limit_bytes=VMEM_LIMIT)


def _rms(x, g):
    r = lax.rsqrt(jnp.mean(x * x, axis=-1, keepdims=True) + EPS)
    return (x * r) * g


def _dot(a, b):
    return jnp.dot(a, b, preferred_element_type=F32)


def _dot_nt(a, b):
    return lax.dot_general(a, b, (((1,), (1,)), ((), ())), preferred_element_type=F32)


def _rms_proj_kernel(x_ref, g_ref, w_ref, wg_ref, o_ref, og_ref, xn_ref):
    @pl.when(pl.program_id(1) == 0)
    def _():
        xn = _rms(x_ref[...], g_ref[...]).astype(BF16)
        xn_ref[...] = xn
        og_ref[...] = _dot(xn, wg_ref[...])

    o_ref[...] = _dot(xn_ref[...], w_ref[...])


def _rms_proj(x2, g, w_all, w_gate, tm=1024, tn=1792):
    t, d = x2.shape
    n = MAIN_COLS
    return pl.pallas_call(
        _rms_proj_kernel,
        grid=(t // tm, n // tn),
        in_specs=[
            pl.BlockSpec((tm, d), lambda i, j: (i, 0)),
            pl.BlockSpec((1, d), lambda i, j: (0, 0)),
            pl.BlockSpec((d, tn), lambda i, j: (0, j)),
            pl.BlockSpec((d, LANES), lambda i, j: (0, 0)),
        ],
        out_specs=[
            pl.BlockSpec((tm, tn), lambda i, j: (i, j)),
            pl.BlockSpec((tm, LANES), lambda i, j: (i, 0)),
        ],
        out_shape=[
            jax.ShapeDtypeStruct((t, n), F32),
            jax.ShapeDtypeStruct((t, LANES), F32),
        ],
        scratch_shapes=[pltpu.VMEM((tm, d), BF16)],
        compiler_params=_params(("parallel", "arbitrary")),
        name="rms_proj",
    )(x2, g, w_all, w_gate)


def _log_sigmoid(x):
    return jnp.minimum(x, 0.0) - jnp.log1p(jnp.exp(-jnp.abs(x)))


def _mix_kernel(bc_ref, cc_ref, uc_ref, q_ref, k_ref, v_ref, o_ref, gc_ref, gr_ref,
                convw_ref, gbc_ref, gbr_ref, ng_ref, *rest, n_tab):
    tab_refs = rest[:n_tab]
    y_ref = rest[n_tab]
    tab_out_refs = rest[n_tab + 1:2 * n_tab + 1]
    caug_ref, m_ref, zc_ref = rest[2 * n_tab + 1:]
    L = MLSTM_CHUNK
    hd = MLSTM_HEAD_DIM

    for src, dst in zip(tab_refs, tab_out_refs):
        dst[...] = src[...].astype(BF16)

    @pl.when(pl.program_id(1) == 0)
    def _():
        caug_ref[...] = jnp.zeros_like(caug_ref)
        m_ref[...] = jnp.zeros_like(m_ref)
        zc_ref[...] = jnp.zeros_like(zc_ref)

    z = cc_ref[...] * uc_ref[...]
    row = lax.broadcasted_iota(jnp.int32, z.shape, 0)
    prev = zc_ref[...]
    p1 = prev[7:8, :]
    p2 = prev[6:7, :]
    z1 = jnp.where(row == 0, p1, pltpu.roll(z, 1, 0))
    z2 = jnp.where(row == 0, p2, jnp.where(row == 1, p1, pltpu.roll(z, 2, 0)))
    cw = convw_ref[...]
    y_conv = bc_ref[...] * (cw[0:1, :] * z2 + cw[1:2, :] * z1 + cw[2:3, :] * z)
    zc_ref[...] = z[L - 8:, :]
    y_ref[:, :CONV_DIM] = y_conv.astype(y_ref.dtype)

    gcol = gc_ref[...] + gbc_ref[...]
    grow = gr_ref[...] + gbr_ref[...]
    ri = lax.broadcasted_iota(jnp.int32, (L, L), 0)
    ci = lax.broadcasted_iota(jnp.int32, (L, L), 1)
    causal = ri >= ci
    tril = causal.astype(F32)
    triu = (ri <= ci).astype(F32)
    hi = lax.Precision.HIGHEST
    b_col_all = jnp.dot(tril, _log_sigmoid(gcol), precision=hi, preferred_element_type=F32)
    b_row_all = jnp.dot(_log_sigmoid(grow), triu, precision=hi, preferred_element_type=F32)
    lane = lax.broadcasted_iota(jnp.int32, (L, LANES), 1)
    ones_col = (lane == 0).astype(BF16)
    scale = hd ** -0.5

    for h in range(MLSTM_HEADS):
        cs = slice(h * hd, (h + 1) * hd)
        qh = q_ref[:, cs].astype(BF16)
        kh = k_ref[:, cs]
        vaug = jnp.concatenate([v_ref[:, cs].astype(BF16), ones_col], axis=1)
        i_row = grow[h:h + 1, :]
        i_col = gcol[:, h:h + 1]
        b_row = b_row_all[MLSTM_HEADS + h:MLSTM_HEADS + h + 1, :]
        b_col = b_col_all[:, MLSTM_HEADS + h:MLSTM_HEADS + h + 1]
        m_prev = m_ref[h:h + 1, 0:1]

        dmat = jnp.where(causal, b_col - b_row + i_row, -jnp.inf)
        inter = b_col + m_prev
        m_t = jnp.maximum(inter, jnp.max(dmat, axis=-1, keepdims=True))
        pw = jnp.exp(dmat - m_t)
        sc = (_dot_nt(qh, kh.astype(BF16)) * scale) * pw
        a = jnp.exp(inter - m_t)
        caug = caug_ref[h]
        nd = a * _dot(qh, caug.astype(BF16)) + _dot(sc.astype(BF16), vaug)
        num = nd[:, :hd]
        den = nd[:, hd:hd + 1]
        hh = num / jnp.maximum(jnp.abs(den), jnp.exp(-m_t))
        hn = _rms(hh, ng_ref[:, cs])
        y_ref[:, CONV_DIM + h * hd:CONV_DIM + (h + 1) * hd] = (
            jax.nn.sigmoid(o_ref[:, cs]) * hn).astype(y_ref.dtype)

        m_new = m_t[L - 1:L, :]
        b_last = b_col[L - 1:L, :]
        w_s = jnp.exp(b_last - b_col + i_col - m_new)
        a_l = jnp.exp(b_last + m_prev - m_new)
        kw = (kh * (w_s * scale)).astype(BF16)
        caug_ref[h] = a_l * caug + _dot(kw.T, vaug)
        m_ref[h:h + 1, :] = jnp.broadcast_to(m_new, (1, LANES))


def _mix(proj, gates_col, gates_row, conv_w, gate_b_col, gate_b_row, norm_g, tables, batch, seq):
    L = MLSTM_CHUNK
    nc = seq // L
    t = batch * seq
    blk = CONV_DIM
    steps = batch * nc
    tab_specs = [pl.BlockSpec((tab.shape[0] // steps, tab.shape[1]), lambda b, c: (b * nc + c, 0))
                 for tab in tables]

    def slab(k):
        return pl.BlockSpec((L, blk), lambda b, c, k=k: (b * nc + c, k))

    return pl.pallas_call(
        functools.partial(_mix_kernel, n_tab=len(tables)),
        grid=(batch, nc),
        in_specs=[slab(0), slab(1), slab(2), slab(3), slab(4), slab(5), slab(6),
                  pl.BlockSpec((L, LANES), lambda b, c: (b * nc + c, 0)),
                  pl.BlockSpec((N_GATES, L), lambda b, c: (0, b * nc + c)),
                  pl.BlockSpec((3, CONV_DIM), lambda b, c: (0, 0)),
                  pl.BlockSpec((1, LANES), lambda b, c: (0, 0)),
                  pl.BlockSpec((N_GATES, 1), lambda b, c: (0, 0)),
                  pl.BlockSpec((1, MLSTM_DIM), lambda b, c: (0, 0)),
                  *tab_specs],
        out_specs=[pl.BlockSpec((L, CONV_DIM + MLSTM_DIM), lambda b, c: (b * nc + c, 0)),
                   *tab_specs],
        out_shape=[jax.ShapeDtypeStruct((t, CONV_DIM + MLSTM_DIM), BF16),
                   *[jax.ShapeDtypeStruct(tab.shape, BF16) for tab in tables]],
        scratch_shapes=[pltpu.VMEM((MLSTM_HEADS, MLSTM_HEAD_DIM, MLSTM_HEAD_DIM + LANES), F32),
                        pltpu.VMEM((8, LANES), F32),
                        pltpu.VMEM((8, CONV_DIM), F32)],
        compiler_params=_params(("parallel", "arbitrary")),
        name="mix",
    )(proj, proj, proj, proj, proj, proj, proj, gates_col, gates_row, conv_w,
      gate_b_col, gate_b_row, norm_g, *tables)


def _matmul_res_kernel(a_ref, w_ref, r_ref, o_ref):
    o_ref[...] = r_ref[...] + _dot(a_ref[...], w_ref[...])


def _matmul_res(a, w, res, tm=512):
    t, k = a.shape
    n = w.shape[1]
    return pl.pallas_call(
        _matmul_res_kernel,
        grid=(t // tm,),
        in_specs=[pl.BlockSpec((tm, k), lambda i: (i, 0)),
                  pl.BlockSpec((k, n), lambda i: (0, 0)),
                  pl.BlockSpec((tm, n), lambda i: (i, 0))],
        out_specs=pl.BlockSpec((tm, n), lambda i: (i, 0)),
        out_shape=jax.ShapeDtypeStruct((t, n), F32),
        compiler_params=_params(("parallel",)),
        name="matmul_res",
    )(a, w, res)


def _topk_rank(s, k):
    n = s.shape[0]
    iota = lax.broadcasted_iota(jnp.int32, s.shape, 0).astype(F32)
    rank = jnp.full(s.shape, float(k), F32)
    vals = []
    for r in range(k):
        mx = jnp.max(s, axis=0, keepdims=True)
        idx = jnp.min(jnp.where(s == mx, iota, float(n)), axis=0, keepdims=True)
        sel = iota == idx
        rank = jnp.where(sel, float(r), rank)
        s = jnp.where(sel, -jnp.inf, s)
        vals.append(mx)
    return rank, vals


def _route_exact(s1, s2):
    K = PEER_TOPK
    rank1, a_vals = _topk_rank(s1, K)
    rank2, b_vals = _topk_rank(s2, K)
    bmat = jnp.concatenate(b_vals, axis=0)
    cmat = jnp.concatenate([a_vals[x] + bmat for x in range(K)], axis=0)
    rank_c, c_vals = _topk_rank(cmat, K)
    chosen = (rank_c < float(K)).astype(F32)
    z = jnp.zeros_like(c_vals[0])
    for v in c_vals:
        z = z + jnp.exp(v - c_vals[0])
    cnt_i = jnp.zeros_like(s1)
    for x in range(K):
        cnt_x = jnp.sum(chosen[x * K:(x + 1) * K, :], axis=0, keepdims=True)
        cnt_i = cnt_i + jnp.where(rank1 == float(x), cnt_x, 0.0)
    return rank2, cnt_i, jnp.exp(s1 - a_vals[0]) / z, jnp.exp(s2 - b_vals[0])


def _sort16_network():
    def merge(lo, hi, r):
        step = r * 2
        if step < hi - lo:
            yield from merge(lo, hi, step)
            yield from merge(lo + r, hi, step)
            yield from [(i, i + r) for i in range(lo + r, hi - r, step)]
        else:
            yield (lo, lo + r)

    def sort(lo, hi):
        if hi - lo >= 1:
            mid = lo + (hi - lo) // 2
            yield from sort(lo, mid)
            yield from sort(mid + 1, hi)
            yield from merge(lo, hi, 1)

    return tuple(sort(0, 15))


_SORT16 = _sort16_network()
SUBLANES = 8


def _top16_sorted(tiles):
    v = list(tiles) + [None] * (16 - len(tiles))
    for i, j in _SORT16:
        a, b = v[i], v[j]
        if b is None:
            continue
        if a is None:
            v[i], v[j] = b, None
        else:
            v[i], v[j] = jnp.maximum(a, b), jnp.minimum(a, b)
    neg = jnp.full(tiles[0].shape, -jnp.inf, F32)
    v = [neg if x is None else x for x in v]
    for shift in (4, 2, 1):
        r = [pltpu.roll(x, shift, 0) for x in v]
        v = [jnp.maximum(v[k], r[15 - k]) for k in range(16)]
        d = 8
        while d >= 1:
            for k in range(16):
                if (k & d) == 0:
                    v[k], v[k + d] = jnp.maximum(v[k], v[k + d]), jnp.minimum(v[k], v[k + d])
            d //= 2
    return v


def _sorted_count(b, pred):
    c8 = pred(b[7])
    c4 = pred(jnp.where(c8, b[11], b[3]))
    c2 = pred(jnp.where(c8, jnp.where(c4, b[13], b[9]), jnp.where(c4, b[5], b[1])))
    hi = jnp.where(c4, jnp.where(c2, b[14], b[12]), jnp.where(c2, b[10], b[8]))
    lo = jnp.where(c4, jnp.where(c2, b[6], b[4]), jnp.where(c2, b[2], b[0]))
    c1 = pred(jnp.where(c8, hi, lo))
    n = (jnp.where(c8, 8.0, 0.0) + jnp.where(c4, 4.0, 0.0)) + (jnp.where(c2, 2.0, 0.0) + jnp.where(c1, 1.0, 0.0))
    return n + jnp.where(pred(b[15]), 1.0, 0.0)


def _route_sorted(s1, s2):
    K = PEER_TOPK
    nk, n = s1.shape
    nt = nk // SUBLANES
    s1t = s1.reshape(nt, SUBLANES, n)
    s2t = s2.reshape(nt, SUBLANES, n)
    a = _top16_sorted([s1t[k] for k in range(nt)])
    b = _top16_sorted([s2t[k] for k in range(nt)])
    sub = lax.broadcasted_iota(jnp.int32, (SUBLANES, n), 0)

    def pack(vals):
        out = vals[0]
        for r in range(1, SUBLANES):
            out = jnp.where(sub == r, vals[r], out)
        return out

    a_lo, a_hi, b_hi = pack(a[:8]), pack(a[8:]), pack(b[8:])
    cands = [a_lo + b[0], a_hi + b[0]]
    for y in range(1, 8):
        cands.append(jnp.where(sub <= K // (y + 1) - 1, a_lo + b[y], -jnp.inf))
    cands.append(a[0] + b_hi)
    cs = _top16_sorted(cands)
    tau = cs[K - 1]
    z = jnp.zeros_like(tau)
    for c in cs:
        z = z + jnp.exp(c - cs[0])
    n3 = jnp.zeros_like(tau)
    for c in cands:
        n3 = n3 + jnp.where(c >= tau, 1.0, 0.0)
    n3 = jnp.sum(n3, axis=0, keepdims=True)

    in1 = s1t >= a[K - 1]
    in2 = s2t >= b[K - 1]
    n1 = jnp.sum(jnp.sum(jnp.where(in1, 1.0, 0.0), axis=0), axis=0, keepdims=True)
    n2 = jnp.sum(jnp.sum(jnp.where(in2, 1.0, 0.0), axis=0), axis=0, keepdims=True)
    bad = jnp.where((n1 != float(K)) | (n2 != float(K)) | (n3 != float(K)), 1.0, 0.0)

    cnt_i = _sorted_count(b, lambda p: s1t + p >= tau)
    r2 = _sorted_count(b, lambda p: p > s2t)
    cnt_i = jnp.where(in1, cnt_i, 0.0)
    ai = jnp.exp(s1t - a[0]) * (1.0 / z)
    bj = jnp.exp(s2t - b[0])
    return (r2.reshape(nk, n), cnt_i.reshape(nk, n), ai.reshape(nk, n), bj.reshape(nk, n), bad)


def _peer_select_kernel(h_ref, g_ref, wq_ref, keys_ref, tab_ref, xn_ref, r2_ref, bj_ref, cnt_ref,
                        ai_ref, tab_out_ref, q_ref):
    hidx = pl.program_id(1)
    tab_out_ref[...] = tab_ref[...].astype(BF16)

    @pl.when(hidx == 0)
    def _():
        xn = _rms(h_ref[...], g_ref[...]).astype(BF16)
        xn_ref[...] = xn
        q = _dot(xn, wq_ref[...]).astype(BF16)
        for g in range(2 * PEER_HEADS):
            q_ref[g] = q[:, g * PEER_HALF:(g + 1) * PEER_HALF]

    hp = r2_ref.shape[0]
    for hh in range(hp):
        head = hp * hidx + hh
        s1 = _dot_nt(keys_ref[hh, 0], q_ref[2 * head])
        s2 = _dot_nt(keys_ref[hh, 1], q_ref[2 * head + 1])

        def emit(r2, cnt_i, ai, bj, hh=hh, head=head):
            r2_ref[hh] = r2.astype(BF16)
            bj_ref[hh] = bj.astype(BF16)
            for k in range(cnt_i.shape[0] // SUBLANES):
                rs = slice(k * SUBLANES, (k + 1) * SUBLANES)
                cnt_ref[pl.ds(k * SUBLANES, SUBLANES), head, :] = cnt_i[rs]
                ai_ref[pl.ds(k * SUBLANES, SUBLANES), head, :] = ai[rs]

        r2, cnt_i, ai, bj, bad = _route_sorted(s1, s2)
        emit(r2, cnt_i, ai, bj)

        @pl.when(jnp.max(bad) > 0.0)
        def _(s1=s1, s2=s2, emit=emit):
            emit(*_route_exact(s1, s2))


def _peer_select(h1, g, wq, keys, table, tt=512):
    t, d = h1.shape
    nk = PEER_N_KEYS
    hp = 2
    steps = (t // tt) * (PEER_HEADS // hp)
    tab_spec = pl.BlockSpec((table.shape[0] // steps, table.shape[1]),
                            lambda i, h: (i * (PEER_HEADS // hp) + h, 0))
    stat_spec = pl.BlockSpec((nk, PEER_HEADS, tt), lambda i, h: (0, 0, i))
    stat16 = jax.ShapeDtypeStruct((PEER_HEADS, nk, t), BF16)
    stat16_spec = pl.BlockSpec((hp, nk, tt), lambda i, h: (h, 0, i))
    return pl.pallas_call(
        _peer_select_kernel,
        grid=(t // tt, PEER_HEADS // hp),
        in_specs=[pl.BlockSpec((tt, d), lambda i, h: (i, 0)),
                  pl.BlockSpec((1, d), lambda i, h: (0, 0)),
                  pl.BlockSpec((d, wq.shape[1]), lambda i, h: (0, 0)),
                  pl.BlockSpec((hp, 2, nk, PEER_HALF), lambda i, h: (h, 0, 0, 0)),
                  tab_spec],
        out_specs=[pl.BlockSpec((tt, d), lambda i, h: (i, 0)),
                   stat16_spec, stat16_spec, stat_spec, stat_spec, tab_spec],
        out_shape=[jax.ShapeDtypeStruct((t, d), BF16), stat16, stat16,
                   jax.ShapeDtypeStruct((nk, PEER_HEADS, t), F32),
                   jax.ShapeDtypeStruct((nk, PEER_HEADS, t), F32),
                   jax.ShapeDtypeStruct(table.shape, BF16)],
        scratch_shapes=[pltpu.VMEM((2 * PEER_HEADS, tt, PEER_HALF), BF16)],
        compiler_params=_params(("parallel", "arbitrary")),
        name="peer_select",
    )(h1, g, wq, keys, table)


def _gelu(a):
    return 0.5 * a * (1.0 + lax.erf(a * (1.0 / math.sqrt(2.0))))


def _peer_dense_kernel(xn_ref, u_ref, v_ref, r2_ref, bj_ref, cnt_ref, ai_ref, o_ref, m_ref):
    @pl.when(pl.program_id(1) == 0)
    def _():
        o_ref[...] = jnp.zeros_like(o_ref)

    tt = xn_ref.shape[0]
    et = u_ref.shape[0]
    nk = PEER_N_KEYS
    rows = 2 * SUBLANES
    tl = 2 * LANES
    zero = jnp.zeros((rows, tl), BF16)
    for il in range(et // nk):
        for tb in range(tt // tl):
            ls = slice(tb * tl, (tb + 1) * tl)
            acc = [None] * (nk // rows)
            for h in range(PEER_HEADS):
                c16 = jnp.broadcast_to(cnt_ref[il, h:h + 1, ls], (rows, tl)).astype(BF16)
                a16 = jnp.broadcast_to(ai_ref[il, h:h + 1, ls], (rows, tl)).astype(BF16)
                for jg in range(nk // rows):
                    js = slice(jg * rows, (jg + 1) * rows)
                    term = jnp.where(r2_ref[h, js, ls] < c16, a16 * bj_ref[h, js, ls], zero)
                    acc[jg] = term if acc[jg] is None else acc[jg] + term
            blk = jnp.concatenate(acc, axis=0).astype(F32)
            m_ref[ls, il * nk:(il + 1) * nk] = blk.T.astype(BF16)
    a = _dot_nt(xn_ref[...], u_ref[...])
    w = _gelu(a).astype(BF16) * m_ref[...]
    o_ref[...] += _dot(w, v_ref[...])


def _peer_dense(xn, u, v, r2, bj, cnt, ai, tt=1024, et=512):
    t, d = xn.shape
    ne = u.shape[0]
    nk = PEER_N_KEYS
    per_j = pl.BlockSpec((PEER_HEADS, nk, tt), lambda i, e: (0, 0, i))
    per_i = pl.BlockSpec((et // nk, PEER_HEADS, tt), lambda i, e: (e, 0, i))
    return pl.pallas_call(
        _peer_dense_kernel,
        grid=(t // tt, ne // et),
        in_specs=[pl.BlockSpec((tt, d), lambda i, e: (i, 0)),
                  pl.BlockSpec((et, d), lambda i, e: (e, 0)),
                  pl.BlockSpec((et, d), lambda i, e: (e, 0)),
                  per_j, per_j, per_i, per_i],
        out_specs=pl.BlockSpec((tt, d), lambda i, e: (i, 0)),
        out_shape=jax.ShapeDtypeStruct((t, d), F32),
        scratch_shapes=[pltpu.VMEM((tt, et), BF16)],
        compiler_params=_params(("parallel", "arbitrary")),
        name="peer_dense",
    )(xn, u, v, r2, bj, cnt, ai)


def _final_kernel(h1_ref, peer_ref, p_ref, g3_ref, wg_ref, wp_ref, gf_ref, o_ref, *, last):
    h2 = h1_ref[...] + peer_ref[...]
    xn = _rms(h2, g3_ref[...]).astype(BF16)
    gate = jax.nn.sigmoid(_dot(xn, wg_ref[...]))
    h3 = h2 + gate * _dot(p_ref[...].astype(BF16), wp_ref[...])
    o_ref[...] = _rms(h3, gf_ref[...]) if last else h3


def _final(h1, peer, p2, g3, wg, wp, gf, last, tm=512):
    t, d = h1.shape
    pd = p2.shape[1]
    row = pl.BlockSpec((tm, d), lambda i: (i, 0))
    vec = pl.BlockSpec((1, d), lambda i: (0, 0))
    return pl.pallas_call(
        functools.partial(_final_kernel, last=last),
        grid=(t // tm,),
        in_specs=[row, row, pl.BlockSpec((tm, pd), lambda i: (i, 0)), vec,
                  pl.BlockSpec((d, d), lambda i: (0, 0)),
                  pl.BlockSpec((pd, d), lambda i: (0, 0)), vec],
        out_specs=row,
        out_shape=jax.ShapeDtypeStruct((t, d), F32),
        compiler_params=_params(("parallel",)),
        name="ple_final",
    )(h1, peer, p2, g3, wg, wp, gf)


def kernel(x, p, norm_mix_g, w_in, conv_w, mlstm_gate_b, mlstm_norm_g, w_out, norm_ffn_g,
           peer_wq, peer_keys, peer_u, peer_v, norm_ple_g, ple_w_gate, ple_w_proj, norm_final_g):
    bsz, seq, d = x.shape
    depth = w_in.shape[0]
    t = bsz * seq
    h = x.reshape(t, d)
    for i in range(depth):
        w_all = w_in[i].astype(BF16)
        w_gates = jnp.pad(w_all[:, MAIN_COLS:], ((0, 0), (0, LANES - N_GATES)))
        proj, gates = _rms_proj(h, norm_mix_g[i].reshape(1, d), w_all, w_gates)
        gate_b = mlstm_gate_b[i].astype(F32)
        mix, u_bf, wo_bf, wq_bf, wg_bf = _mix(
            proj, gates, gates[:, :N_GATES].T, conv_w[i],
            jnp.pad(gate_b, (0, LANES - N_GATES)).reshape(1, LANES),
            gate_b.reshape(N_GATES, 1), mlstm_norm_g[i].reshape(1, MLSTM_DIM),
            (peer_u[i], w_out[i], peer_wq[i], ple_w_gate[i]), bsz, seq)
        h1 = _matmul_res(mix, wo_bf, h)
        xn, r2, bj, cnt, ai, v_bf = _peer_select(h1, norm_ffn_g[i].reshape(1, d), wq_bf,
                                                 peer_keys[i].astype(BF16), peer_v[i])
        peer = _peer_dense(xn, u_bf, v_bf, r2, bj, cnt, ai)
        h = _final(h1, peer, p[i].reshape(t, -1), norm_ple_g[i].reshape(1, d),
                   wg_bf, ple_w_proj[i].astype(BF16),
                   norm_final_g.reshape(1, d), i == depth - 1)
    return h.reshape(bsz, seq, d)
```

```python
import functools
import math

import jax
import jax.numpy as jnp
from jax import lax
from jax.experimental import pallas as pl
from jax.experimental.pallas import tpu as pltpu

F32 = jnp.float32
BF16 = jnp.bfloat16

CONV_DIM = 1024
MLSTM_HEADS = 4
MLSTM_HEAD_DIM = 256
MLSTM_DIM = MLSTM_HEADS * MLSTM_HEAD_DIM
MLSTM_CHUNK = 128
MAIN_COLS = 3 * CONV_DIM + 4 * MLSTM_DIM
N_GATES = 2 * MLSTM_HEADS
PEER_HEADS = 8
PEER_N_KEYS = 128
PEER_HALF = 128
PEER_TOPK = 16
EPS = 1e-6

LANES = 128
VMEM_LIMIT = 60 * 1024 * 1024


def _params(semantics):
    return pltpu.CompilerParams(dimension_semantics=semantics, vmem_limit_bytes=VMEM_LIMIT)


def _rms(x, g):
    r = lax.rsqrt(jnp.mean(x * x, axis=-1, keepdims=True) + EPS)
    return (x * r) * g


def _dot(a, b):
    return jnp.dot(a, b, preferred_element_type=F32)


def _dot_nt(a, b):
    return lax.dot_general(a, b, (((1,), (1,)), ((), ())), preferred_element_type=F32)


def _rms_proj_kernel(x_ref, g_ref, w_ref, wg_ref, o_ref, og_ref, xn_ref):
    @pl.when(pl.program_id(1) == 0)
    def _():
        xn = _rms(x_ref[...], g_ref[...]).astype(BF16)
        xn_ref[...] = xn
        og_ref[...] = _dot(xn, wg_ref[...])

    o_ref[...] = _dot(xn_ref[...], w_ref[...])


def _rms_proj(x2, g, w_all, w_gate, tm=1024, tn=1792):
    t, d = x2.shape
    n = MAIN_COLS
    return pl.pallas_call(
        _rms_proj_kernel,
        grid=(t // tm, n // tn),
        in_specs=[
            pl.BlockSpec((tm, d), lambda i, j: (i, 0)),
            pl.BlockSpec((1, d), lambda i, j: (0, 0)),
            pl.BlockSpec((d, tn), lambda i, j: (0, j)),
            pl.BlockSpec((d, LANES), lambda i, j: (0, 0)),
        ],
        out_specs=[
            pl.BlockSpec((tm, tn), lambda i, j: (i, j)),
            pl.BlockSpec((tm, LANES), lambda i, j: (i, 0)),
        ],
        out_shape=[
            jax.ShapeDtypeStruct((t, n), F32),
            jax.ShapeDtypeStruct((t, LANES), F32),
        ],
        scratch_shapes=[pltpu.VMEM((tm, d), BF16)],
        compiler_params=_params(("parallel", "arbitrary")),
        name="rms_proj",
    )(x2, g, w_all, w_gate)


def _log_sigmoid(x):
    return jnp.minimum(x, 0.0) - jnp.log1p(jnp.exp(-jnp.abs(x)))


def _mix_kernel(bc_ref, cc_ref, uc_ref, q_ref, k_ref, v_ref, o_ref, gc_ref, gr_ref,
                convw_ref, gbc_ref, gbr_ref, ng_ref, *rest, n_tab):
    tab_refs = rest[:n_tab]
    y_ref = rest[n_tab]
    tab_out_refs = rest[n_tab + 1:2 * n_tab + 1]
    caug_ref, m_ref, zc_ref = rest[2 * n_tab + 1:]
    L = MLSTM_CHUNK
    hd = MLSTM_HEAD_DIM

    for src, dst in zip(tab_refs, tab_out_refs):
        dst[...] = src[...].astype(BF16)

    @pl.when(pl.program_id(1) == 0)
    def _():
        caug_ref[...] = jnp.zeros_like(caug_ref)
        m_ref[...] = jnp.zeros_like(m_ref)
        zc_ref[...] = jnp.zeros_like(zc_ref)

    z = cc_ref[...] * uc_ref[...]
    row = lax.broadcasted_iota(jnp.int32, z.shape, 0)
    prev = zc_ref[...]
    p1 = prev[7:8, :]
    p2 = prev[6:7, :]
    z1 = jnp.where(row == 0, p1, pltpu.roll(z, 1, 0))
    z2 = jnp.where(row == 0, p2, jnp.where(row == 1, p1, pltpu.roll(z, 2, 0)))
    cw = convw_ref[...]
    y_conv = bc_ref[...] * (cw[0:1, :] * z2 + cw[1:2, :] * z1 + cw[2:3, :] * z)
    zc_ref[...] = z[L - 8:, :]
    y_ref[:, :CONV_DIM] = y_conv.astype(y_ref.dtype)

    gcol = gc_ref[...] + gbc_ref[...]
    grow = gr_ref[...] + gbr_ref[...]
    ri = lax.broadcasted_iota(jnp.int32, (L, L), 0)
    ci = lax.broadcasted_iota(jnp.int32, (L, L), 1)
    causal = ri >= ci
    tril = causal.astype(F32)
    triu = (ri <= ci).astype(F32)
    hi = lax.Precision.HIGHEST
    b_col_all = jnp.dot(tril, _log_sigmoid(gcol), precision=hi, preferred_element_type=F32)
    b_row_all = jnp.dot(_log_sigmoid(grow), triu, precision=hi, preferred_element_type=F32)
    lane = lax.broadcasted_iota(jnp.int32, (L, LANES), 1)
    ones_col = (lane == 0).astype(BF16)
    scale = hd ** -0.5

    for h in range(MLSTM_HEADS):
        cs = slice(h * hd, (h + 1) * hd)
        qh = q_ref[:, cs].astype(BF16)
        kh = k_ref[:, cs]
        vaug = jnp.concatenate([v_ref[:, cs].astype(BF16), ones_col], axis=1)
        i_row = grow[h:h + 1, :]
        i_col = gcol[:, h:h + 1]
        b_row = b_row_all[MLSTM_HEADS + h:MLSTM_HEADS + h + 1, :]
        b_col = b_col_all[:, MLSTM_HEADS + h:MLSTM_HEADS + h + 1]
        m_prev = m_ref[h:h + 1, 0:1]

        dmat = jnp.where(causal, b_col - b_row + i_row, -jnp.inf)
        inter = b_col + m_prev
        m_t = jnp.maximum(inter, jnp.max(dmat, axis=-1, keepdims=True))
        pw = jnp.exp(dmat - m_t)
        sc = (_dot_nt(qh, kh.astype(BF16)) * scale) * pw
        a = jnp.exp(inter - m_t)
        caug = caug_ref[h]
        nd = a * _dot(qh, caug.astype(BF16)) + _dot(sc.astype(BF16), vaug)
        num = nd[:, :hd]
        den = nd[:, hd:hd + 1]
        hh = num / jnp.maximum(jnp.abs(den), jnp.exp(-m_t))
        hn = _rms(hh, ng_ref[:, cs])
        y_ref[:, CONV_DIM + h * hd:CONV_DIM + (h + 1) * hd] = (
            jax.nn.sigmoid(o_ref[:, cs]) * hn).astype(y_ref.dtype)

        m_new = m_t[L - 1:L, :]
        b_last = b_col[L - 1:L, :]
        w_s = jnp.exp(b_last - b_col + i_col - m_new)
        a_l = jnp.exp(b_last + m_prev - m_new)
        kw = (kh * (w_s * scale)).astype(BF16)
        caug_ref[h] = a_l * caug + _dot(kw.T, vaug)
        m_ref[h:h + 1, :] = jnp.broadcast_to(m_new, (1, LANES))


def _mix(proj, gates_col, gates_row, conv_w, gate_b_col, gate_b_row, norm_g, tables, batch, seq):
    L = MLSTM_CHUNK
    nc = seq // L
    t = batch * seq
    blk = CONV_DIM
    steps = batch * nc
    tab_specs = [pl.BlockSpec((tab.shape[0] // steps, tab.shape[1]), lambda b, c: (b * nc + c, 0))
                 for tab in tables]

    def slab(k):
        return pl.BlockSpec((L, blk), lambda b, c, k=k: (b * nc + c, k))

    return pl.pallas_call(
        functools.partial(_mix_kernel, n_tab=len(tables)),
        grid=(batch, nc),
        in_specs=[slab(0), slab(1), slab(2), slab(3), slab(4), slab(5), slab(6),
                  pl.BlockSpec((L, LANES), lambda b, c: (b * nc + c, 0)),
                  pl.BlockSpec((N_GATES, L), lambda b, c: (0, b * nc + c)),
                  pl.BlockSpec((3, CONV_DIM), lambda b, c: (0, 0)),
                  pl.BlockSpec((1, LANES), lambda b, c: (0, 0)),
                  pl.BlockSpec((N_GATES, 1), lambda b, c: (0, 0)),
                  pl.BlockSpec((1, MLSTM_DIM), lambda b, c: (0, 0)),
                  *tab_specs],
        out_specs=[pl.BlockSpec((L, CONV_DIM + MLSTM_DIM), lambda b, c: (b * nc + c, 0)),
                   *tab_specs],
        out_shape=[jax.ShapeDtypeStruct((t, CONV_DIM + MLSTM_DIM), BF16),
                   *[jax.ShapeDtypeStruct(tab.shape, BF16) for tab in tables]],
        scratch_shapes=[pltpu.VMEM((MLSTM_HEADS, MLSTM_HEAD_DIM, MLSTM_HEAD_DIM + LANES), F32),
                        pltpu.VMEM((8, LANES), F32),
                        pltpu.VMEM((8, CONV_DIM), F32)],
        compiler_params=_params(("parallel", "arbitrary")),
        name="mix",
    )(proj, proj, proj, proj, proj, proj, proj, gates_col, gates_row, conv_w,
      gate_b_col, gate_b_row, norm_g, *tables)


def _matmul_res_kernel(a_ref, w_ref, r_ref, o_ref):
    o_ref[...] = r_ref[...] + _dot(a_ref[...], w_ref[...])


def _matmul_res(a, w, res, tm=512):
    t, k = a.shape
    n = w.shape[1]
    return pl.pallas_call(
        _matmul_res_kernel,
        grid=(t // tm,),
        in_specs=[pl.BlockSpec((tm, k), lambda i: (i, 0)),
                  pl.BlockSpec((k, n), lambda i: (0, 0)),
                  pl.BlockSpec((tm, n), lambda i: (i, 0))],
        out_specs=pl.BlockSpec((tm, n), lambda i: (i, 0)),
        out_shape=jax.ShapeDtypeStruct((t, n), F32),
        compiler_params=_params(("parallel",)),
        name="matmul_res",
    )(a, w, res)


def _topk_rank(s, k):
    n = s.shape[0]
    iota = lax.broadcasted_iota(jnp.int32, s.shape, 0).astype(F32)
    rank = jnp.full(s.shape, float(k), F32)
    vals = []
    for r in range(k):
        mx = jnp.max(s, axis=0, keepdims=True)
        idx = jnp.min(jnp.where(s == mx, iota, float(n)), axis=0, keepdims=True)
        sel = iota == idx
        rank = jnp.where(sel, float(r), rank)
        s = jnp.where(sel, -jnp.inf, s)
        vals.append(mx)
    return rank, vals


def _route_exact(s1, s2):
    K = PEER_TOPK
    rank1, a_vals = _topk_rank(s1, K)
    rank2, b_vals = _topk_rank(s2, K)
    bmat = jnp.concatenate(b_vals, axis=0)
    cmat = jnp.concatenate([a_vals[x] + bmat for x in range(K)], axis=0)
    rank_c, c_vals = _topk_rank(cmat, K)
    chosen = (rank_c < float(K)).astype(F32)
    z = jnp.zeros_like(c_vals[0])
    for v in c_vals:
        z = z + jnp.exp(v - c_vals[0])
    cnt_i = jnp.zeros_like(s1)
    for x in range(K):
        cnt_x = jnp.sum(chosen[x * K:(x + 1) * K, :], axis=0, keepdims=True)
        cnt_i = cnt_i + jnp.where(rank1 == float(x), cnt_x, 0.0)
    return rank2, cnt_i, jnp.exp(s1 - a_vals[0]) / z, jnp.exp(s2 - b_vals[0])


def _sort16_network():
    def merge(lo, hi, r):
        step = r * 2
        if step < hi - lo:
            yield from merge(lo, hi, step)
            yield from merge(lo + r, hi, step)
            yield from [(i, i + r) for i in range(lo + r, hi - r, step)]
        else:
            yield (lo, lo + r)

    def sort(lo, hi):
        if hi - lo >= 1:
            mid = lo + (hi - lo) // 2
            yield from sort(lo, mid)
            yield from sort(mid + 1, hi)
            yield from merge(lo, hi, 1)

    return tuple(sort(0, 15))


_SORT16 = _sort16_network()
SUBLANES = 8


def _top16_sorted(tiles):
    v = list(tiles) + [None] * (16 - len(tiles))
    for i, j in _SORT16:
        a, b = v[i], v[j]
        if b is None:
            continue
        if a is None:
            v[i], v[j] = b, None
        else:
            v[i], v[j] = jnp.maximum(a, b), jnp.minimum(a, b)
    neg = jnp.full(tiles[0].shape, -jnp.inf, F32)
    v = [neg if x is None else x for x in v]
    for shift in (4, 2, 1):
        r = [pltpu.roll(x, shift, 0) for x in v]
        v = [jnp.maximum(v[k], r[15 - k]) for k in range(16)]
        d = 8
        while d >= 1:
            for k in range(16):
                if (k & d) == 0:
                    v[k], v[k + d] = jnp.maximum(v[k], v[k + d]), jnp.minimum(v[k], v[k + d])
            d //= 2
    return v


def _sorted_count(b, pred):
    c8 = pred(b[7])
    c4 = pred(jnp.where(c8, b[11], b[3]))
    c2 = pred(jnp.where(c8, jnp.where(c4, b[13], b[9]), jnp.where(c4, b[5], b[1])))
    hi = jnp.where(c4, jnp.where(c2, b[14], b[12]), jnp.where(c2, b[10], b[8]))
    lo = jnp.where(c4, jnp.where(c2, b[6], b[4]), jnp.where(c2, b[2], b[0]))
    c1 = pred(jnp.where(c8, hi, lo))
    n = (jnp.where(c8, 8.0, 0.0) + jnp.where(c4, 4.0, 0.0)) + (jnp.where(c2, 2.0, 0.0) + jnp.where(c1, 1.0, 0.0))
    return n + jnp.where(pred(b[15]), 1.0, 0.0)


def _route_sorted(s1, s2):
    K = PEER_TOPK
    nk, n = s1.shape
    nt = nk // SUBLANES
    s1t = s1.reshape(nt, SUBLANES, n)
    s2t = s2.reshape(nt, SUBLANES, n)
    a = _top16_sorted([s1t[k] for k in range(nt)])
    b = _top16_sorted([s2t[k] for k in range(nt)])
    sub = lax.broadcasted_iota(jnp.int32, (SUBLANES, n), 0)

    def pack(vals):
        out = vals[0]
        for r in range(1, SUBLANES):
            out = jnp.where(sub == r, vals[r], out)
        return out

    a_lo, a_hi, b_hi = pack(a[:8]), pack(a[8:]), pack(b[8:])
    cands = [a_lo + b[0], a_hi + b[0]]
    for y in range(1, 8):
        cands.append(jnp.where(sub <= K // (y + 1) - 1, a_lo + b[y], -jnp.inf))
    cands.append(a[0] + b_hi)
    cs = _top16_sorted(cands)
    tau = cs[K - 1]
    z = jnp.zeros_like(tau)
    for c in cs:
        z = z + jnp.exp(c - cs[0])
    n3 = jnp.zeros_like(tau)
    for c in cands:
        n3 = n3 + jnp.where(c >= tau, 1.0, 0.0)
    n3 = jnp.sum(n3, axis=0, keepdims=True)

    in1 = s1t >= a[K - 1]
    in2 = s2t >= b[K - 1]
    n1 = jnp.sum(jnp.sum(jnp.where(in1, 1.0, 0.0), axis=0), axis=0, keepdims=True)
    n2 = jnp.sum(jnp.sum(jnp.where(in2, 1.0, 0.0), axis=0), axis=0, keepdims=True)
    bad = jnp.where((n1 != float(K)) | (n2 != float(K)) | (n3 != float(K)), 1.0, 0.0)

    cnt_i = _sorted_count(b, lambda p: s1t + p >= tau)
    r2 = _sorted_count(b, lambda p: p > s2t)
    cnt_i = jnp.where(in1, cnt_i, 0.0)
    ai = jnp.exp(s1t - a[0]) * (1.0 / z)
    bj = jnp.exp(s2t - b[0])
    return (r2.reshape(nk, n), cnt_i.reshape(nk, n), ai.reshape(nk, n), bj.reshape(nk, n), bad)


def _peer_select_kernel(h_ref, g_ref, wq_ref, keys_ref, tab_ref, xn_ref, r2_ref, bj_ref, cnt_ref,
                        ai_ref, tab_out_ref, q_ref):
    hidx = pl.program_id(1)
    tab_out_ref[...] = tab_ref[...].astype(BF16)

    @pl.when(hidx == 0)
    def _():
        xn = _rms(h_ref[...], g_ref[...]).astype(BF16)
        xn_ref[...] = xn
        q = _dot(xn, wq_ref[...]).astype(BF16)
        for g in range(2 * PEER_HEADS):
            q_ref[g] = q[:, g * PEER_HALF:(g + 1) * PEER_HALF]

    hp = r2_ref.shape[0]

    def emit(hh, r2, cnt_i, ai, bj):
        head = hp * hidx + hh
        r2_ref[hh] = r2.astype(BF16)
        bj_ref[hh] = bj.astype(BF16)
        for k in range(cnt_i.shape[0] // SUBLANES):
            rs = slice(k * SUBLANES, (k + 1) * SUBLANES)
            cnt_ref[pl.ds(k * SUBLANES, SUBLANES), head, :] = cnt_i[rs]
            ai_ref[pl.ds(k * SUBLANES, SUBLANES), head, :] = ai[rs]

    scores = []
    bad = None
    for hh in range(hp):
        head = hp * hidx + hh
        s1 = _dot_nt(keys_ref[hh, 0], q_ref[2 * head])
        s2 = _dot_nt(keys_ref[hh, 1], q_ref[2 * head + 1])
        scores.append((s1, s2))
        r2, cnt_i, ai, bj, bad_h = _route_sorted(s1, s2)
        emit(hh, r2, cnt_i, ai, bj)
        bad = bad_h if bad is None else jnp.maximum(bad, bad_h)

    @pl.when(jnp.max(bad) > 0.0)
    def _():
        for hh, (s1, s2) in enumerate(scores):
            emit(hh, *_route_exact(s1, s2))


def _peer_select(h1, g, wq, keys, table, tt=512):
    t, d = h1.shape
    nk = PEER_N_KEYS
    hp = 2
    steps = (t // tt) * (PEER_HEADS // hp)
    tab_spec = pl.BlockSpec((table.shape[0] // steps, table.shape[1]),
                            lambda i, h: (i * (PEER_HEADS // hp) + h, 0))
    stat_spec = pl.BlockSpec((nk, PEER_HEADS, tt), lambda i, h: (0, 0, i))
    stat16 = jax.ShapeDtypeStruct((PEER_HEADS, nk, t), BF16)
    stat16_spec = pl.BlockSpec((hp, nk, tt), lambda i, h: (h, 0, i))
    return pl.pallas_call(
        _peer_select_kernel,
        grid=(t // tt, PEER_HEADS // hp),
        in_specs=[pl.BlockSpec((tt, d), lambda i, h: (i, 0)),
                  pl.BlockSpec((1, d), lambda i, h: (0, 0)),
                  pl.BlockSpec((d, wq.shape[1]), lambda i, h: (0, 0)),
                  pl.BlockSpec((hp, 2, nk, PEER_HALF), lambda i, h: (h, 0, 0, 0)),
                  tab_spec],
        out_specs=[pl.BlockSpec((tt, d), lambda i, h: (i, 0)),
                   stat16_spec, stat16_spec, stat_spec, stat_spec, tab_spec],
        out_shape=[jax.ShapeDtypeStruct((t, d), BF16), stat16, stat16,
                   jax.ShapeDtypeStruct((nk, PEER_HEADS, t), F32),
                   jax.ShapeDtypeStruct((nk, PEER_HEADS, t), F32),
                   jax.ShapeDtypeStruct(table.shape, BF16)],
        scratch_shapes=[pltpu.VMEM((2 * PEER_HEADS, tt, PEER_HALF), BF16)],
        compiler_params=_params(("parallel", "arbitrary")),
        name="peer_select",
    )(h1, g, wq, keys, table)


def _gelu(a):
    return 0.5 * a * (1.0 + lax.erf(a * (1.0 / math.sqrt(2.0))))


def _peer_dense_kernel(xn_ref, u_ref, v_ref, r2_ref, bj_ref, cnt_ref, ai_ref, o_ref, m_ref):
    @pl.when(pl.program_id(1) == 0)
    def _():
        o_ref[...] = jnp.zeros_like(o_ref)

    tt = xn_ref.shape[0]
    et = u_ref.shape[0]
    nk = PEER_N_KEYS
    rows = 2 * SUBLANES
    tl = 2 * LANES
    zero = jnp.zeros((rows, tl), BF16)
    ni = et // nk
    for tb in range(tt // tl):
        ls = slice(tb * tl, (tb + 1) * tl)
        acc = [[None] * (nk // rows) for _ in range(ni)]
        ctile = [cnt_ref[il, :, ls] for il in range(ni)]
        atile = [ai_ref[il, :, ls] for il in range(ni)]
        for h in range(PEER_HEADS):
            c16 = [jnp.broadcast_to(ctile[il][h:h + 1, :], (rows, tl)).astype(BF16) for il in range(ni)]
            a16 = [jnp.broadcast_to(atile[il][h:h + 1, :], (rows, tl)).astype(BF16) for il in range(ni)]
            for jg in range(nk // rows):
                js = slice(jg * rows, (jg + 1) * rows)
                r2 = r2_ref[h, js, ls]
                bj = bj_ref[h, js, ls]
                for il in range(ni):
                    term = jnp.where(r2 < c16[il], a16[il] * bj, zero)
                    acc[il][jg] = term if acc[il][jg] is None else acc[il][jg] + term
        for il in range(ni):
            blk = jnp.concatenate(acc[il], axis=0).astype(F32)
            m_ref[ls, il * nk:(il + 1) * nk] = blk.T.astype(BF16)
    a = _dot_nt(xn_ref[...], u_ref[...])
    w = _gelu(a).astype(BF16) * m_ref[...]
    o_ref[...] += _dot(w, v_ref[...])


def _peer_dense(xn, u, v, r2, bj, cnt, ai, tt=1024, et=512):
    t, d = xn.shape
    ne = u.shape[0]
    nk = PEER_N_KEYS
    per_j = pl.BlockSpec((PEER_HEADS, nk, tt), lambda i, e: (0, 0, i))
    per_i = pl.BlockSpec((et // nk, PEER_HEADS, tt), lambda i, e: (e, 0, i))
    return pl.pallas_call(
        _peer_dense_kernel,
        grid=(t // tt, ne // et),
        in_specs=[pl.BlockSpec((tt, d), lambda i, e: (i, 0)),
                  pl.BlockSpec((et, d), lambda i, e: (e, 0)),
                  pl.BlockSpec((et, d), lambda i, e: (e, 0)),
                  per_j, per_j, per_i, per_i],
        out_specs=pl.BlockSpec((tt, d), lambda i, e: (i, 0)),
        out_shape=jax.ShapeDtypeStruct((t, d), F32),
        scratch_shapes=[pltpu.VMEM((tt, et), BF16)],
        compiler_params=_params(("parallel", "arbitrary")),
        name="peer_dense",
    )(xn, u, v, r2, bj, cnt, ai)


def _final_kernel(h1_ref, peer_ref, p_ref, g3_ref, wg_ref, wp_ref, gf_ref, o_ref, *, last):
    h2 = h1_ref[...] + peer_ref[...]
    xn = _rms(h2, g3_ref[...]).astype(BF16)
    gate = jax.nn.sigmoid(_dot(xn, wg_ref[...]))
    h3 = h2 + gate * _dot(p_ref[...].astype(BF16), wp_ref[...])
    o_ref[...] = _rms(h3, gf_ref[...]) if last else h3


def _final(h1, peer, p2, g3, wg, wp, gf, last, tm=512):
    t, d = h1.shape
    pd = p2.shape[1]
    row = pl.BlockSpec((tm, d), lambda i: (i, 0))
    vec = pl.BlockSpec((1, d), lambda i: (0, 0))
    return pl.pallas_call(
        functools.partial(_final_kernel, last=last),
        grid=(t // tm,),
        in_specs=[row, row, pl.BlockSpec((tm, pd), lambda i: (i, 0)), vec,
                  pl.BlockSpec((d, d), lambda i: (0, 0)),
                  pl.BlockSpec((pd, d), lambda i: (0, 0)), vec],
        out_specs=row,
        out_shape=jax.ShapeDtypeStruct((t, d), F32),
        compiler_params=_params(("parallel",)),
        name="ple_final",
    )(h1, peer, p2, g3, wg, wp, gf)


def kernel(x, p, norm_mix_g, w_in, conv_w, mlstm_gate_b, mlstm_norm_g, w_out, norm_ffn_g,
           peer_wq, peer_keys, peer_u, peer_v, norm_ple_g, ple_w_gate, ple_w_proj, norm_final_g):
    bsz, seq, d = x.shape
    depth = w_in.shape[0]
    t = bsz * seq
    h = x.reshape(t, d)
    for i in range(depth):
        w_all = w_in[i].astype(BF16)
        w_gates = jnp.pad(w_all[:, MAIN_COLS:], ((0, 0), (0, LANES - N_GATES)))
        proj, gates = _rms_proj(h, norm_mix_g[i].reshape(1, d), w_all, w_gates)
        gate_b = mlstm_gate_b[i].astype(F32)
        mix, u_bf, wo_bf, wq_bf, wg_bf = _mix(
            proj, gates, gates[:, :N_GATES].T, conv_w[i],
            jnp.pad(gate_b, (0, LANES - N_GATES)).reshape(1, LANES),
            gate_b.reshape(N_GATES, 1), mlstm_norm_g[i].reshape(1, MLSTM_DIM),
            (peer_u[i], w_out[i], peer_wq[i], ple_w_gate[i]), bsz, seq)
        h1 = _matmul_res(mix, wo_bf, h)
        xn, r2, bj, cnt, ai, v_bf = _peer_select(h1, norm_ffn_g[i].reshape(1, d), wq_bf,
                                                 peer_keys[i].astype(BF16), peer_v[i])
        peer = _peer_dense(xn, u_bf, v_bf, r2, bj, cnt, ai)
        h = _final(h1, peer, p[i].reshape(t, -1), norm_ple_g[i].reshape(1, d),
                   wg_bf, ple_w_proj[i].astype(BF16),
                   norm_final_g.reshape(1, d), i == depth - 1)
    return h.reshape(bsz, seq, d)
```
